```python
import math
import jax, jax.numpy as jnp
from jax import lax
import numpy as np

D_MODEL = 1024
BATCH = 4
SEQ = 8192
DEPTH = 4

CHUNK = 64
Q_BLOCK = 128
HEAD_DIM = 64
FOX_HEADS = 6
SB_HEADS = 6
FOX_WIDTH = FOX_HEADS * HEAD_DIM
SB_WIDTH = SB_HEADS * HEAD_DIM
MIX_WIDTH = D_MODEL
SSM_WIDTH = MIX_WIDTH - FOX_WIDTH - SB_WIDTH
SSM_GROUP = 16
SSM_GROUPS = SSM_WIDTH // SSM_GROUP
SSM_STATE = 64
D_FF = 3584
N_EXPERTS = 8
TOP_K = 2
N_DENSE_LAYERS = (DEPTH + 1) // 2
N_MOE_LAYERS = DEPTH // 2
EPS = 1e-6

PROJ_SIZES = (
    FOX_WIDTH, FOX_WIDTH, FOX_WIDTH, FOX_HEADS,
    SB_WIDTH, SB_WIDTH, SB_WIDTH,
    SSM_WIDTH,
)
PROJ_WIDTH = sum(PROJ_SIZES)
PROJ_SPLITS = [int(v) for v in np.cumsum(PROJ_SIZES)[:-1]]

kernel_name = "hybrid_fox_stickbreak_s5_moe_trunk"


def rms_norm(x, gain):
    xf = x.astype(jnp.float32)
    y = xf * lax.rsqrt(jnp.mean(xf * xf, axis=-1, keepdims=True) + EPS)
    return (y * gain.astype(jnp.float32)).astype(x.dtype)


def head_rms(t, gain=None):
    tf = t.astype(jnp.float32)
    y = tf * lax.rsqrt(jnp.mean(tf * tf, axis=-1, keepdims=True) + EPS)
    if gain is not None:
        y = y * gain.astype(jnp.float32)
    return y.astype(t.dtype)


def group_rms(y, group):
    b, s, w = y.shape
    return head_rms(y.reshape(b, s, w // group, group)).reshape(b, s, w)


def to_query_blocks(t):
    b, h, s = t.shape[:3]
    t = t.reshape((b, h, s // Q_BLOCK, Q_BLOCK) + t.shape[3:])
    return jnp.moveaxis(t, 2, 0)


def from_query_blocks(o):
    nb, b, h, q, d = o.shape
    return o.transpose(1, 0, 3, 2, 4).reshape(b, nb * q, h * d)


def fox_attention(q, k, v, log_f, q_gain, k_gain):
    s_len = q.shape[1]
    scale = HEAD_DIM ** -0.5
    q = head_rms(q, q_gain)
    k = head_rms(k, k_gain)
    c = jnp.cumsum(log_f, axis=1).transpose(0, 2, 1)
    qh = q.transpose(0, 2, 1, 3)
    kh = k.transpose(0, 2, 1, 3)
    vh = v.transpose(0, 2, 1, 3)
    key_pos = jnp.arange(s_len)
    starts = jnp.arange(s_len // Q_BLOCK) * Q_BLOCK

    def block(args):
        qb, cb, start = args
        qpos = start + jnp.arange(Q_BLOCK)
        logits = jnp.einsum('bhqd,bhkd->bhqk', qb, kh,
                            preferred_element_type=jnp.float32) * scale
        logits = logits + cb[..., None] - c[:, :, None, :]
        mask = key_pos[None, :] <= qpos[:, None]
        p = jax.nn.softmax(jnp.where(mask, logits, -jnp.inf), axis=-1)
        return jnp.einsum('bhqk,bhkd->bhqd', p.astype(vh.dtype), vh)

    out = lax.map(block, (to_query_blocks(qh), to_query_blocks(c), starts))
    return from_query_blocks(out)


def stick_breaking_attention(q, k, v):
    s_len = q.shape[1]
    scale = HEAD_DIM ** -0.5
    qh = q.transpose(0, 2, 1, 3)
    kh = k.transpose(0, 2, 1, 3)
    vh = v.transpose(0, 2, 1, 3)
    key_pos = jnp.arange(s_len)
    starts = jnp.arange(s_len // Q_BLOCK) * Q_BLOCK

    def block(args):
        qb, start = args
        qpos = start + jnp.arange(Q_BLOCK)
        z = jnp.einsum('bhqd,bhkd->bhqk', qb, kh,
                       preferred_element_type=jnp.float32) * scale
        mask = key_pos[None, :] < qpos[:, None]
        log_not = jnp.where(mask, jax.nn.log_sigmoid(-z), 0.0)
        later = lax.cumsum(log_not, axis=3, reverse=True) - log_not
        a = jnp.where(mask, jnp.exp(jax.nn.log_sigmoid(z) + later), 0.0)
        return jnp.einsum('bhqk,bhkd->bhqd', a.astype(vh.dtype), vh)

    out = lax.map(block, (to_query_blocks(qh), starts))
    return from_query_blocks(out)


def _complex_affine_combine(e1, e2):
    ar1, ai1, br1, bi1 = e1
    ar2, ai2, br2, bi2 = e2
    ar = ar2 * ar1 - ai2 * ai1
    ai = ar2 * ai1 + ai2 * ar1
    br = ar2 * br1 - ai2 * bi1 + br2
    bi = ar2 * bi1 + ai2 * br1 + bi2
    return (ar, ai, br, bi)


def s5_mixer(u, a_re, a_im, log_dt, b_re, b_im, c_re, c_im, d_skip, w_glu, b_glu):
    f32 = jnp.float32
    bsz, s_len, _ = u.shape
    uf = u.astype(f32)
    ug = uf.reshape(bsz, s_len, SSM_GROUPS, SSM_GROUP)
    a_re = a_re.astype(f32); a_im = a_im.astype(f32)
    dt = jnp.exp(log_dt.astype(f32))[:, None]
    mag = jnp.exp(dt * a_re)
    ang = dt * a_im
    abar_re = mag * jnp.cos(ang)
    abar_im = mag * jnp.sin(ang)
    num_re = abar_re - 1.0
    num_im = abar_im
    den = a_re * a_re + a_im * a_im
    coef_re = (num_re * a_re + num_im * a_im) / den
    coef_im = (num_im * a_re - num_re * a_im) / den
    b_re = b_re.astype(f32); b_im = b_im.astype(f32)
    bbar_re = coef_re[..., None] * b_re - coef_im[..., None] * b_im
    bbar_im = coef_re[..., None] * b_im + coef_im[..., None] * b_re
    bu_re = jnp.einsum('gnc,bsgc->bsgn', bbar_re, ug)
    bu_im = jnp.einsum('gnc,bsgc->bsgn', bbar_im, ug)
    ar = jnp.broadcast_to(abar_re, bu_re.shape)
    ai = jnp.broadcast_to(abar_im, bu_re.shape)
    _, _, h_re, h_im = lax.associative_scan(
        _complex_affine_combine, (ar, ai, bu_re, bu_im), axis=1)
    y = (jnp.einsum('gcn,bsgn->bsgc', c_re.astype(f32), h_re)
         - jnp.einsum('gcn,bsgn->bsgc', c_im.astype(f32), h_im))
    y = y.reshape(bsz, s_len, SSM_WIDTH) + d_skip.astype(f32) * uf
    g = jax.nn.gelu(y)
    g = g * jax.nn.sigmoid(g @ w_glu.astype(f32) + b_glu.astype(f32))
    return g.astype(u.dtype)


def swiglu(h, w_gate, w_up, w_down):
    return (jax.nn.silu(h @ w_gate) * (h @ w_up)) @ w_down


def moe_swiglu(h, w_router, w_gate, w_up, w_down):
    shape = h.shape
    hf = h.reshape(-1, shape[-1])
    logits = (hf @ w_router).astype(jnp.float32)
    top_val, top_idx = lax.top_k(logits, TOP_K)
    top_w = jax.nn.softmax(top_val, axis=-1)
    gates = jnp.sum(jax.nn.one_hot(top_idx, N_EXPERTS, dtype=jnp.float32)
                    * top_w[..., None], axis=1)
    out = jnp.zeros_like(hf)
    for e in range(N_EXPERTS):
        y = swiglu(hf, w_gate[e], w_up[e], w_down[e])
        out = out + gates[:, e:e + 1].astype(hf.dtype) * y
    return out.reshape(shape)


def setup_inputs(seed: int = 0) -> dict:
    key = jax.random.key(seed)
    ks = jax.random.split(key, 32)
    f32 = jnp.float32

    def nrm(k, shape, scale):
        return jax.random.normal(k, shape, f32) * scale

    G, N = SSM_GROUPS, SSM_STATE
    n_idx = jnp.arange(N, dtype=f32)
    return {
        "x": nrm(ks[0], (BATCH, SEQ, D_MODEL), 1.0),
        "attn_norm": 1.0 + nrm(ks[1], (DEPTH, D_MODEL), 0.02),
        "ffn_norm": 1.0 + nrm(ks[2], (DEPTH, D_MODEL), 0.02),
        "w_in": nrm(ks[3], (DEPTH, D_MODEL, PROJ_WIDTH), D_MODEL ** -0.5),
        "b_forget": jax.random.uniform(ks[4], (DEPTH, FOX_HEADS), f32, 1.0, 5.0),
        "fox_q_norm": 1.0 + nrm(ks[5], (DEPTH, HEAD_DIM), 0.02),
        "fox_k_norm": 1.0 + nrm(ks[6], (DEPTH, HEAD_DIM), 0.02),
        "ssm_a_re": -0.5 + nrm(ks[7], (DEPTH, G, N), 0.01),
        "ssm_a_im": jnp.pi * n_idx + nrm(ks[8], (DEPTH, G, N), 0.01),
        "ssm_log_dt": jax.random.uniform(ks[9], (DEPTH, G), f32,
                                         math.log(1e-3), math.log(1e-1)),
        "ssm_b_re": nrm(ks[10], (DEPTH, G, N, SSM_GROUP), (2 * SSM_GROUP) ** -0.5),
        "ssm_b_im": nrm(ks[11], (DEPTH, G, N, SSM_GROUP), (2 * SSM_GROUP) ** -0.5),
        "ssm_c_re": nrm(ks[12], (DEPTH, G, SSM_GROUP, N), N ** -0.5),
        "ssm_c_im": nrm(ks[13], (DEPTH, G, SSM_GROUP, N), N ** -0.5),
        "ssm_d": nrm(ks[14], (DEPTH, SSM_WIDTH), 1.0),
        "ssm_w_glu": nrm(ks[15], (DEPTH, SSM_WIDTH, SSM_WIDTH), SSM_WIDTH ** -0.5),
        "ssm_b_glu": nrm(ks[16], (DEPTH, SSM_WIDTH), 0.02),
        "out_norm": 1.0 + nrm(ks[17], (DEPTH, MIX_WIDTH), 0.02),
        "w_out": nrm(ks[18], (DEPTH, MIX_WIDTH, D_MODEL), MIX_WIDTH ** -0.5),
        "dense_w_gate": nrm(ks[19], (N_DENSE_LAYERS, D_MODEL, D_FF), D_MODEL ** -0.5),
        "dense_w_up": nrm(ks[20], (N_DENSE_LAYERS, D_MODEL, D_FF), D_MODEL ** -0.5),
        "dense_w_down": nrm(ks[21], (N_DENSE_LAYERS, D_FF, D_MODEL), D_FF ** -0.5),
        "moe_w_router": nrm(ks[22], (N_MOE_LAYERS, D_MODEL, N_EXPERTS), D_MODEL ** -0.5),
        "moe_w_gate": nrm(ks[23], (N_MOE_LAYERS, N_EXPERTS, D_MODEL, D_FF), D_MODEL ** -0.5),
        "moe_w_up": nrm(ks[24], (N_MOE_LAYERS, N_EXPERTS, D_MODEL, D_FF), D_MODEL ** -0.5),
        "moe_w_down": nrm(ks[25], (N_MOE_LAYERS, N_EXPERTS, D_FF, D_MODEL), D_FF ** -0.5),
    }


def reference(x, attn_norm, ffn_norm, w_in, b_forget, fox_q_norm, fox_k_norm,
              ssm_a_re, ssm_a_im, ssm_log_dt, ssm_b_re, ssm_b_im, ssm_c_re,
              ssm_c_im, ssm_d, ssm_w_glu, ssm_b_glu, out_norm, w_out,
              dense_w_gate, dense_w_up, dense_w_down, moe_w_router,
              moe_w_gate, moe_w_up, moe_w_down):
    bsz, s_len, _ = x.shape
    for i in range(DEPTH):
        h = rms_norm(x, attn_norm[i])
        proj = h @ w_in[i]
        fq, fk, fv, fgl, sq, sk, sv, su = jnp.split(proj, PROJ_SPLITS, axis=-1)
        log_f = jax.nn.log_sigmoid((fgl + b_forget[i]).astype(jnp.float32))
        heads_f = (bsz, s_len, FOX_HEADS, HEAD_DIM)
        heads_s = (bsz, s_len, SB_HEADS, HEAD_DIM)
        y_fox = fox_attention(fq.reshape(heads_f), fk.reshape(heads_f),
                              fv.reshape(heads_f), log_f,
                              fox_q_norm[i], fox_k_norm[i])
        y_sb = stick_breaking_attention(sq.reshape(heads_s), sk.reshape(heads_s),
                                        sv.reshape(heads_s))
        y_ssm = s5_mixer(su, ssm_a_re[i], ssm_a_im[i], ssm_log_dt[i],
                         ssm_b_re[i], ssm_b_im[i], ssm_c_re[i], ssm_c_im[i],
                         ssm_d[i], ssm_w_glu[i], ssm_b_glu[i])
        mix = jnp.concatenate([group_rms(y_fox, HEAD_DIM),
                               group_rms(y_sb, HEAD_DIM),
                               group_rms(y_ssm, SSM_GROUP)], axis=-1)
        mix = mix * out_norm[i].astype(mix.dtype)
        x = x + mix @ w_out[i]
        h = rms_norm(x, ffn_norm[i])
        j = i // 2
        if i % 2 == 0:
            x = x + swiglu(h, dense_w_gate[j], dense_w_up[j], dense_w_down[j])
        else:
            x = x + moe_swiglu(h, moe_w_router[j], moe_w_gate[j],
                               moe_w_up[j], moe_w_down[j])
    return x
```

```python
import functools
import math

import jax
import jax.numpy as jnp
import numpy as np
from jax import lax
from jax.experimental import pallas as pl
from jax.experimental.pallas import tpu as pltpu

F32 = jnp.float32
BF16 = jnp.bfloat16

EPS = 1e-6
LANES = 128
HEAD_DIM = 64
N_HEADS = 6
N_PAIRS = N_HEADS // 2
ATTN_WIDTH = N_HEADS * HEAD_DIM
SSM_GROUP = 16
SSM_GROUPS = 16
SSM_STATE = 64
SSM_WIDTH = SSM_GROUP * SSM_GROUPS
SSM_LANES = SSM_GROUPS * SSM_STATE
N_EXPERTS = 8
TOP_K = 2
QK_SCALE = HEAD_DIM ** -0.5
NEG_BIG = -1e30
LOG_ZERO = -104.0
VMEM_LIMIT = 56 * 1024 * 1024

TM_PROJ = 512
TQ = 512
SSM_CHUNK = 256
TM_FFN = 1024
TF_FFN = 512
TR_GATHER = 512
TC_COMBINE = 256


def _dot(a, b):
    return jnp.dot(a, b, preferred_element_type=F32)


def _dot_nt(a, b):
    return lax.dot_general(a, b, (((1,), (1,)), ((), ())), preferred_element_type=F32)


def _split2(x):
    hi = x.astype(BF16)
    lo = (x - hi.astype(F32)).astype(BF16)
    return hi, lo


def _split3(x):
    hi = x.astype(BF16)
    r = x - hi.astype(F32)
    mid = r.astype(BF16)
    lo = (r - mid.astype(F32)).astype(BF16)
    return hi, mid, lo


def _softplus(z):
    return jnp.maximum(z, 0.0) + jnp.log1p(jnp.exp(-jnp.abs(z)))


def _cparams(sem):
    return pltpu.CompilerParams(dimension_semantics=sem, vmem_limit_bytes=VMEM_LIMIT)


def _inproj_kernel(x_ref, g_ref, wf_ref, ws_ref, wu_ref, wgh_ref, wgl_ref, bf_ref,
                   ltri_ref, bd_ref, gq_ref, gk_ref, selq_ref, selk_ref, cq_ref, ck_ref,
                   qa_ref, ka_ref, va_ref, sq_ref, sk_ref, sv_ref, su_ref, c_ref,
                   carry_ref):
    tm = x_ref.shape[1]

    @pl.when(pl.program_id(1) == 0)
    def _():
        carry_ref[...] = jnp.zeros_like(carry_ref)

    x = x_ref[0]
    h = x * lax.rsqrt(jnp.mean(x * x, axis=-1, keepdims=True) + EPS) * g_ref[...]
    hb, hl = _split2(h)
    pf = _dot(hb, wf_ref[...])
    ps = _dot(hb, ws_ref[...])
    su_ref[0] = _dot(hb, wu_ref[...])

    fgl = _dot(hb, wgh_ref[...]) + _dot(hb, wgl_ref[...]) + _dot(hl, wgh_ref[...])
    lane = lax.broadcasted_iota(jnp.int32, (tm, LANES), 1)
    lf = jnp.where(lane < N_HEADS, -_softplus(-(fgl + bf_ref[...])), 0.0)
    lt = ltri_ref[...]
    l_hi, l_mid, l_lo = _split3(lf)
    c = _dot(lt, l_hi) + _dot(lt, l_mid) + _dot(lt, l_lo) + carry_ref[...]
    carry_ref[...] = c[tm - 1:tm, :]
    c_ref[0] = c

    cparts = jnp.concatenate(_split3(c), axis=1)
    augq = _dot(cparts, selq_ref[...]) + cq_ref[...]
    augk = _dot(cparts, selk_ref[...]) + ck_ref[...]

    w = ATTN_WIDTH
    fq, fk, fv = pf[:, 0:w], pf[:, w:2 * w], pf[:, 2 * w:3 * w]
    bd = bd_ref[...]
    qn = fq * lax.rsqrt(_dot((fq * fq).astype(BF16), bd) + EPS) * gq_ref[...]
    kn = fk * lax.rsqrt(_dot((fk * fk).astype(BF16), bd) + EPS) * gk_ref[...]
    sq, sk, sv = ps[:, 0:w] * QK_SCALE, ps[:, w:2 * w], ps[:, 2 * w:3 * w]

    for h_idx in range(N_HEADS):
        p, hh = divmod(h_idx, 2)
        in_head = (lane < HEAD_DIM) if hh == 0 else (lane >= HEAD_DIM)
        one_lane = HEAD_DIM if hh == 0 else 0
        ts = slice(LANES * p, LANES * (p + 1))
        ah = slice(LANES * h_idx, LANES * (h_idx + 1))
        qa_ref[0, h_idx] = jnp.where(in_head, qn[:, ts], augq[:, ah]).astype(BF16)
        ka_ref[0, h_idx] = jnp.where(in_head, kn[:, ts], augk[:, ah]).astype(BF16)
        ones_col = jnp.where(lane == one_lane, 1.0, 0.0)
        va_ref[0, h_idx] = jnp.where(in_head, fv[:, ts], ones_col).astype(BF16)
        sq_ref[0, h_idx] = jnp.where(in_head, sq[:, ts], 0.0).astype(BF16)
        sk_ref[0, h_idx] = jnp.where(in_head, sk[:, ts], 0.0).astype(BF16)
        sv_ref[0, h_idx] = jnp.where(in_head, sv[:, ts], 0.0).astype(BF16)


def _aug_tables():
    selq = np.zeros((3 * LANES, N_HEADS * LANES), np.float32)
    selk = np.zeros((3 * LANES, N_HEADS * LANES), np.float32)
    cq = np.zeros((1, N_HEADS * LANES), np.float32)
    ck = np.zeros((1, N_HEADS * LANES), np.float32)
    for h in range(N_HEADS):
        base = LANES * h + (HEAD_DIM if h % 2 == 0 else 0)
        for piece in range(3):
            selq[piece * LANES + h, base + piece] = 1.0
            selk[piece * LANES + h, base + 3 + piece] = -1.0
            cq[0, base + 3 + piece] = 1.0
            ck[0, base + piece] = 1.0
    return selq, selk, cq, ck


def _inproj(x, gain, w_in, b_forget, q_gain, k_gain):
    bsz, s_len, d = x.shape
    tm = TM_PROJ
    w = ATTN_WIDTH
    o_fg = 3 * w
    o_s = o_fg + N_HEADS
    o_u = o_s + 3 * w
    wf = w_in[:, :o_fg].astype(BF16)
    ws = w_in[:, o_s:o_u].astype(BF16)
    wu = w_in[:, o_u:].astype(BF16)
    wg = jnp.pad(w_in[:, o_fg:o_s], ((0, 0), (0, LANES - N_HEADS)))
    wgh, wgl = _split2(wg)
    bfp = jnp.pad(b_forget.astype(F32), (0, LANES - N_HEADS))[None, :]
    ltri = jnp.asarray(np.tril(np.ones((tm, tm), np.float32)), BF16)
    bd = jnp.asarray(np.kron(np.eye(N_HEADS, dtype=np.float32),
                             np.full((HEAD_DIM, HEAD_DIM), 1.0 / HEAD_DIM, np.float32)), BF16)
    gq = (jnp.tile(q_gain.astype(F32), N_HEADS) * QK_SCALE)[None, :]
    gk = jnp.tile(k_gain.astype(F32), N_HEADS)[None, :]
    selq, selk, cq, ck = _aug_tables()
    selq, selk = jnp.asarray(selq, BF16), jnp.asarray(selk, BF16)
    cq, ck = jnp.asarray(cq), jnp.asarray(ck)

    def full(a):
        return pl.BlockSpec(a.shape, lambda b, i: (0,) * a.ndim)

    consts = (gain.astype(F32)[None, :], wf, ws, wu, wgh, wgl, bfp, ltri, bd, gq, gk,
              selq, selk, cq, ck)
    head_shape = jax.ShapeDtypeStruct((bsz, N_HEADS, s_len, LANES), BF16)
    head_spec = pl.BlockSpec((1, N_HEADS, tm, LANES), lambda b, i: (b, 0, i, 0))
    return pl.pallas_call(
        _inproj_kernel,
        grid=(bsz, s_len // tm),
        in_specs=[pl.BlockSpec((1, tm, d), lambda b, i: (b, i, 0))] + [full(a) for a in consts],
        out_specs=[head_spec] * 6 + [
            pl.BlockSpec((1, tm, SSM_WIDTH), lambda b, i: (b, i, 0)),
            pl.BlockSpec((1, tm, LANES), lambda b, i: (b, i, 0))],
        out_shape=[head_shape] * 6 + [
            jax.ShapeDtypeStruct((bsz, s_len, SSM_WIDTH), F32),
            jax.ShapeDtypeStruct((bsz, s_len, LANES), F32)],
        scratch_shapes=[pltpu.VMEM((1, LANES), F32)],
        compiler_params=_cparams(("arbitrary", "arbitrary")),
        name="inproj",
    )(x, *consts)


def _pair_rms(y, gain):
    lane = lax.broadcasted_iota(jnp.int32, y.shape, 1)
    lo = lane < HEAD_DIM
    y2 = y * y
    ms0 = jnp.sum(jnp.where(lo, y2, 0.0), axis=-1, keepdims=True) * (1.0 / HEAD_DIM)
    ms1 = jnp.sum(jnp.where(lo, 0.0, y2), axis=-1, keepdims=True) * (1.0 / HEAD_DIM)
    return y * jnp.where(lo, lax.rsqrt(ms0 + EPS), lax.rsqrt(ms1 + EPS)) * gain


def _fox_kernel(kvs_ref, qa_ref, ka_ref, va_ref, gain_ref, o_ref, acc_ref, m_ref, *, nq):
    tq = qa_ref.shape[2]
    b, p, i = pl.program_id(0), pl.program_id(1), pl.program_id(2)
    row = lax.broadcasted_iota(jnp.int32, (tq, tq), 0)
    col = lax.broadcasted_iota(jnp.int32, (tq, tq), 1)

    for hh in range(2):
        q = qa_ref[0, hh]
        start = kvs_ref[(b * N_HEADS + 2 * p + hh) * nq + i]
        m_ref[hh] = jnp.full((tq, 1), NEG_BIG, F32)
        acc_ref[hh] = jnp.zeros((tq, LANES), F32)

        def step(j, masked, hh=hh, q=q):
            off = pl.multiple_of(j * tq, tq)
            k = ka_ref[0, hh, pl.ds(off, tq), :]
            v = va_ref[0, hh, pl.ds(off, tq), :]
            s = _dot_nt(q, k)
            if masked:
                s = jnp.where(col <= row, s, NEG_BIG)
            m_prev = m_ref[hh]
            m_new = jnp.maximum(m_prev, jnp.max(s, axis=-1, keepdims=True))
            pexp = jnp.exp(s - m_new)
            acc_ref[hh] = jnp.exp(m_prev - m_new) * acc_ref[hh] + _dot(pexp.astype(BF16), v)
            m_ref[hh] = m_new

        def body(j, carry):
            step(j, False)
            return carry

        lax.fori_loop(start, i, body, 0)
        step(i, True)

    lane = lax.broadcasted_iota(jnp.int32, (tq, LANES), 1)
    a0, a1 = acc_ref[0], acc_ref[1]
    y = jnp.where(lane < HEAD_DIM, a0 / a0[:, HEAD_DIM:HEAD_DIM + 1], a1 / a1[:, 0:1])
    o_ref[0] = _pair_rms(y, gain_ref[0]).astype(BF16)


def _fox(qa, ka, va, kv_start, gain3):
    bsz, _, s_len, _ = qa.shape
    tq = TQ
    nq = s_len // tq
    kv_spec = pl.BlockSpec((1, 2, s_len, LANES), lambda b, p, i, kvs: (b, p, 0, 0))
    return pl.pallas_call(
        functools.partial(_fox_kernel, nq=nq),
        grid_spec=pltpu.PrefetchScalarGridSpec(
            num_scalar_prefetch=1,
            grid=(bsz, N_PAIRS, nq),
            in_specs=[pl.BlockSpec((1, 2, tq, LANES), lambda b, p, i, kvs: (b, p, i, 0)),
                      kv_spec, kv_spec,
                      pl.BlockSpec((1, 1, LANES), lambda b, p, i, kvs: (p, 0, 0))],
            out_specs=pl.BlockSpec((1, tq, LANES), lambda b, p, i, kvs: (b, i, p)),
            scratch_shapes=[pltpu.VMEM((2, tq, LANES), F32), pltpu.VMEM((2, tq, 1), F32)]),
        out_shape=jax.ShapeDtypeStruct((bsz, s_len, ATTN_WIDTH), BF16),
        compiler_params=_cparams(("arbitrary", "arbitrary", "arbitrary")),
        name="fox_attention",
    )(kv_start, qa, ka, va, gain3)


def _fox_kv_start(c, q_gain, k_gain, s_len):
    tq = TQ
    nq = s_len // tq
    bound = 1.05 * HEAD_DIM * QK_SCALE * jnp.max(jnp.abs(q_gain)) * jnp.max(jnp.abs(k_gain))
    ch = c[:, :, :N_HEADS]
    c_first = ch[:, 0::tq, :]
    c_end = ch[:, tq - 1::tq, :]
    skip = (c_first[:, :, None, :] - c_end[:, None, :, :]) < (LOG_ZERO - 2.0 * bound)
    start = jnp.sum(skip.astype(jnp.int32), axis=2)
    start = jnp.minimum(start, jnp.arange(nq, dtype=jnp.int32)[None, :, None])
    return start.transpose(0, 2, 1).reshape(-1)


def _sb_kernel(q_ref, k_ref, v_ref, u_ref, gain_ref, o_ref, acc_ref, r_ref):
    tq = q_ref.shape[2]
    i = pl.program_id(2)
    row = lax.broadcasted_iota(jnp.int32, (tq, tq), 0)
    col = lax.broadcasted_iota(jnp.int32, (tq, tq), 1)

    for hh in range(2):
        q = q_ref[0, hh]
        acc_ref[hh] = jnp.zeros((tq, LANES), F32)
        r_ref[hh] = jnp.zeros((tq, 1), F32)

        def step(j, masked, hh=hh, q=q):
            off = pl.multiple_of(j * tq, tq)
            k = k_ref[0, hh, pl.ds(off, tq), :]
            v = v_ref[0, hh, pl.ds(off, tq), :]
            z = _dot_nt(q, k)
            sp = _softplus(z)
            log_not = jnp.where(col < row, -sp, 0.0) if masked else -sp
            n_hi, n_lo = _split2(log_not)
            u = u_ref[...]
            later_in = _dot(n_hi, u) + _dot(n_lo, u)
            r_prev = r_ref[hh]
            a = jnp.exp(z - sp + later_in + r_prev)
            if masked:
                a = jnp.where(col < row, a, 0.0)
            acc_ref[hh] += _dot(a.astype(BF16), v)
            r_new = r_prev + later_in[:, 0:1] + log_not[:, 0:1]
            r_ref[hh] = r_new
            return jnp.max(r_new)

        def cond(carry):
            j, r_max = carry
            return jnp.logical_and(j >= 0, r_max >= LOG_ZERO)

        def body(carry):
            j, _ = carry
            return j - 1, step(j, False)

        lax.while_loop(cond, body, (i - 1, step(i, True)))

    lane = lax.broadcasted_iota(jnp.int32, (tq, LANES), 1)
    y = jnp.where(lane < HEAD_DIM, acc_ref[0], acc_ref[1])
    o_ref[0] = _pair_rms(y, gain_ref[0]).astype(BF16)


def _sb(sq, sk, sv, gain3):
    bsz, _, s_len, _ = sq.shape
    tq = TQ
    upper = jnp.asarray(np.tril(np.ones((tq, tq), np.float32), -1), BF16)
    kv_spec = pl.BlockSpec((1, 2, s_len, LANES), lambda b, p, i: (b, p, 0, 0))
    return pl.pallas_call(
        _sb_kernel,
        grid=(bsz, N_PAIRS, s_len // tq),
        in_specs=[pl.BlockSpec((1, 2, tq, LANES), lambda b, p, i: (b, p, i, 0)),
                  kv_spec, kv_spec,
                  pl.BlockSpec((tq, tq), lambda b, p, i: (0, 0)),
                  pl.BlockSpec((1, 1, LANES), lambda b, p, i: (p, 0, 0))],
        out_specs=pl.BlockSpec((1, tq, LANES), lambda b, p, i: (b, i, p)),
        out_shape=jax.ShapeDtypeStruct((bsz, s_len, ATTN_WIDTH), BF16),
        scratch_shapes=[pltpu.VMEM((2, tq, LANES), F32), pltpu.VMEM((2, tq, 1), F32)],
        compiler_params=_cparams(("arbitrary", "arbitrary", "arbitrary")),
        name="stickbreak_attention",
    )(sq, sk, sv, upper, gain3)


def _ssm_kernel(u_ref, bre_ref, bim_ref, ccat_ref, pw_ref, d_ref, wglu_ref, bglu_ref,
                bd_ref, gain_ref, o_ref, xr_ref, xi_ref, hc_ref):
    chunk = u_ref.shape[1]

    @pl.when(pl.program_id(1) == 0)
    def _():
        hc_ref[...] = jnp.zeros_like(hc_ref)

    u = u_ref[0]
    ub = u.astype(BF16)
    xr_ref[...] = _dot(ub, bre_ref[...])
    xi_ref[...] = _dot(ub, bim_ref[...])

    def group(g, carry):
        hr, hi = carry
        rows = pl.ds(pl.multiple_of(g * 8, 8), 8)
        xr, xi = xr_ref[rows, :], xi_ref[rows, :]
        for k, d in enumerate((1, 2, 4)):
            mr, mi = pw_ref[2 + 2 * k], pw_ref[3 + 2 * k]
            sr, si = pltpu.roll(xr, d, 0), pltpu.roll(xi, d, 0)
            xr, xi = xr + mr * sr - mi * si, xi + mr * si + mi * sr
        pr, pi = pw_ref[0], pw_ref[1]
        xr, xi = xr + pr * hr - pi * hi, xi + pr * hi + pi * hr
        xr_ref[rows, :] = xr
        xi_ref[rows, :] = xi
        return xr[7:8, :], xi[7:8, :]

    hr, hi = lax.fori_loop(0, chunk // 8, group, (hc_ref[0:1, :], hc_ref[1:2, :]))
    hc_ref[0:1, :] = hr
    hc_ref[1:2, :] = hi

    hcat = jnp.concatenate([xr_ref[...], xi_ref[...]], axis=1).astype(BF16)
    y = _dot(hcat, ccat_ref[...]) + d_ref[...] * u
    g = 0.5 * y * (1.0 + jnp.tanh(math.sqrt(2.0 / math.pi) * (y + 0.044715 * (y * y * y))))
    g = g * jax.nn.sigmoid(_dot(g.astype(BF16), wglu_ref[...]) + bglu_ref[...])
    ms = _dot((g * g).astype(BF16), bd_ref[...])
    o_ref[0] = (g * lax.rsqrt(ms + EPS) * gain_ref[...]).astype(BF16)


def _ssm_tables(a_re, a_im, log_dt, b_re, b_im, c_re, c_im):
    a_re = a_re.astype(F32)
    a_im = a_im.astype(F32)
    dt = jnp.exp(log_dt.astype(F32))[:, None]
    mag = jnp.exp(dt * a_re)
    ang = dt * a_im
    abar_re = mag * jnp.cos(ang)
    abar_im = mag * jnp.sin(ang)
    num_re = abar_re - 1.0
    num_im = abar_im
    den = a_re * a_re + a_im * a_im
    coef_re = (num_re * a_re + num_im * a_im) / den
    coef_im = (num_im * a_re - num_re * a_im) / den
    b_re = b_re.astype(F32)
    b_im = b_im.astype(F32)
    bbar_re = coef_re[..., None] * b_re - coef_im[..., None] * b_im
    bbar_im = coef_re[..., None] * b_im + coef_im[..., None] * b_re
    eye = jnp.eye(SSM_GROUPS, dtype=F32)

    def b_blockdiag(bb):
        return jnp.einsum('gnc,gh->gchn', bb, eye).reshape(SSM_WIDTH, SSM_LANES).astype(BF16)

    def c_blockdiag(cc):
        return jnp.einsum('gcn,gh->gnhc', cc.astype(F32), eye).reshape(SSM_LANES, SSM_WIDTH)

    ccat = jnp.concatenate([c_blockdiag(c_re), -c_blockdiag(c_im)], axis=0).astype(BF16)

    ar = abar_re.reshape(1, SSM_LANES)
    ai = abar_im.reshape(1, SSM_LANES)
    pows = [(ar, ai)]
    for _ in range(7):
        pr, pi = pows[-1]
        pows.append((pr * ar - pi * ai, pr * ai + pi * ar))
    p_re = jnp.concatenate([p[0] for p in pows], axis=0)
    p_im = jnp.concatenate([p[1] for p in pows], axis=0)
    rows = jnp.arange(8)[:, None]
    tabs = [p_re, p_im]
    for d in (1, 2, 4):
        tabs.append(jnp.where(rows >= d, pows[d - 1][0], 0.0))
        tabs.append(jnp.where(rows >= d, pows[d - 1][1], 0.0))
    return b_blockdiag(bbar_re), b_blockdiag(bbar_im), ccat, jnp.stack(tabs, axis=0)


def _ssm(su, tables, d_skip, w_glu, b_glu, gain):
    bsz, s_len, _ = su.shape
    chunk = SSM_CHUNK
    bre, bim, ccat, pw = tables
    bd = jnp.asarray(np.kron(np.eye(SSM_GROUPS, dtype=np.float32),
                             np.full((SSM_GROUP, SSM_GROUP), 1.0 / SSM_GROUP, np.float32)), BF16)
    consts = (bre, bim, ccat, pw, d_skip.astype(F32)[None, :], w_glu.astype(BF16),
              b_glu.astype(F32)[None, :], bd, gain.astype(F32)[None, :])

    def full(a):
        return pl.BlockSpec(a.shape, lambda b, i: (0,) * a.ndim)

    return pl.pallas_call(
        _ssm_kernel,
        grid=(bsz, s_len // chunk),
        in_specs=[pl.BlockSpec((1, chunk, SSM_WIDTH), lambda b, i: (b, i, 0))]
        + [full(a) for a in consts],
        out_specs=pl.BlockSpec((1, chunk, SSM_WIDTH), lambda b, i: (b, i, 0)),
        out_shape=jax.ShapeDtypeStruct((bsz, s_len, SSM_WIDTH), BF16),
        scratch_shapes=[pltpu.VMEM((chunk, SSM_LANES), F32), pltpu.VMEM((chunk, SSM_LANES), F32),
                        pltpu.VMEM((2, SSM_LANES), F32)],
        compiler_params=_cparams(("arbitrary", "arbitrary")),
        name="s5_mixer",
    )(su, *consts)


def _outproj_kernel(*refs, routed):
    if routed:
        yf_ref, ys_ref, ym_ref, x_ref, wo_ref, g_ref, wrh_ref, wrl_ref, xo_ref, h_ref, rt_ref = refs
    else:
        yf_ref, ys_ref, ym_ref, x_ref, wo_ref, g_ref, xo_ref, h_ref = refs
    mix = jnp.concatenate([yf_ref[...], ys_ref[...], ym_ref[...]], axis=1)
    x = x_ref[...] + _dot(mix, wo_ref[...])
    xo_ref[...] = x
    h = x * lax.rsqrt(jnp.mean(x * x, axis=-1, keepdims=True) + EPS) * g_ref[...]
    if not routed:
        h_ref[...] = h.astype(BF16)
        return
    h_ref[...] = h
    hb, hl = _split2(h)
    logits = _dot(hb, wrh_ref[...]) + _dot(hb, wrl_ref[...]) + _dot(hl, wrh_ref[...])
    lane = lax.broadcasted_iota(jnp.int32, logits.shape, 1).astype(F32)
    logits = jnp.where(lane < N_EXPERTS, logits, NEG_BIG)
    m1 = jnp.max(logits, axis=-1, keepdims=True)
    i1 = jnp.min(jnp.where(logits == m1, lane, float(LANES)), axis=-1, keepdims=True)
    rest = jnp.where(lane == i1, NEG_BIG, logits)
    m2 = jnp.max(rest, axis=-1, keepdims=True)
    i2 = jnp.min(jnp.where(rest == m2, lane, float(LANES)), axis=-1, keepdims=True)
    e2 = jnp.exp(m2 - m1)
    w1 = 1.0 / (1.0 + e2)
    w2 = e2 / (1.0 + e2)
    rt_ref[...] = jnp.where(lane == 0, i1, jnp.where(lane == 1, i2, jnp.where(
        lane == 2, w1, jnp.where(lane == 3, w2, 0.0))))


def _outproj(yf, ys, ym, x, w_out, ffn_gain, w_router=None):
    t, d = x.shape
    tm = TM_PROJ
    routed = w_router is not None
    consts = [w_out.astype(BF16), ffn_gain.astype(F32)[None, :]]
    if routed:
        consts += list(_split2(jnp.pad(w_router.astype(F32), ((0, 0), (0, LANES - N_EXPERTS)))))

    def rows(width):
        return pl.BlockSpec((tm, width), lambda i: (i, 0))

    def full(a):
        return pl.BlockSpec(a.shape, lambda i: (0,) * a.ndim)

    out_specs = [rows(d), rows(d)]
    out_shape = [jax.ShapeDtypeStruct((t, d), F32), jax.ShapeDtypeStruct((t, d), F32 if routed else BF16)]
    if routed:
        out_specs.append(rows(LANES))
        out_shape.append(jax.ShapeDtypeStruct((t, LANES), F32))
    return pl.pallas_call(
        functools.partial(_outproj_kernel, routed=routed),
        grid=(t // tm,),
        in_specs=[rows(ATTN_WIDTH), rows(ATTN_WIDTH), rows(SSM_WIDTH), rows(d)]
        + [full(a) for a in consts],
        out_specs=out_specs,
        out_shape=out_shape,
        compiler_params=_cparams(("arbitrary",)),
        name="outproj_routed" if routed else "outproj",
    )(yf, ys, ym, x, *consts)


def _swiglu_chunk(h, wg_ref, wu_ref, wd_ref):
    g = _dot(h, wg_ref[...])
    u = _dot(h, wu_ref[...])
    return _dot((g * jax.nn.sigmoid(g) * u).astype(BF16), wd_ref[...])


def _ffn_kernel(h_ref, x_ref, wg_ref, wu_ref, wd_ref, o_ref):
    @pl.when(pl.program_id(1) == 0)
    def _():
        o_ref[...] = x_ref[...]

    o_ref[...] += _swiglu_chunk(h_ref[...], wg_ref, wu_ref, wd_ref)


def _ffn(h, x, w_gate, w_up, w_down):
    t, d = x.shape
    d_ff = w_gate.shape[1]
    tm, tf = TM_FFN, TF_FFN
    return pl.pallas_call(
        _ffn_kernel,
        grid=(t // tm, d_ff // tf),
        in_specs=[pl.BlockSpec((tm, d), lambda i, f: (i, 0)),
                  pl.BlockSpec((tm, d), lambda i, f: (i, 0)),
                  pl.BlockSpec((d, tf), lambda i, f: (0, f)),
                  pl.BlockSpec((d, tf), lambda i, f: (0, f)),
                  pl.BlockSpec((tf, d), lambda i, f: (f, 0))],
        out_specs=pl.BlockSpec((tm, d), lambda i, f: (i, 0)),
        out_shape=jax.ShapeDtypeStruct((t, d), F32),
        compiler_params=_cparams(("arbitrary", "arbitrary")),
        name="dense_swiglu",
    )(h, x, w_gate.astype(BF16), w_up.astype(BF16), w_down.astype(BF16))


def _row_copy(src_hbm, dst_ref, src_row, dst_row, sem):
    return pltpu.make_async_copy(src_hbm.at[pl.ds(src_row, 1), :],
                                 dst_ref.at[pl.ds(dst_row, 1), :], sem)


def _gather_rows(idx_ref, src_hbm, dst_ref, sem, n_rows):
    def start(r, carry):
        _row_copy(src_hbm, dst_ref, idx_ref[0, 0, r], r, sem).start()
        return carry

    def wait(r, carry):
        _row_copy(src_hbm, dst_ref, idx_ref[0, 0, r], r, sem).wait()
        return carry

    lax.fori_loop(0, n_rows, start, 0)
    lax.fori_loop(0, n_rows, wait, 0)


def _gather_kernel(idx_ref, h_hbm, o_ref, sem):
    _gather_rows(idx_ref, h_hbm, o_ref, sem, o_ref.shape[0])


def _moe_gather(h, src_tok):
    n_rows = src_tok.shape[0]
    d = h.shape[1]
    tr = TR_GATHER
    return pl.pallas_call(
        _gather_kernel,
        grid=(n_rows // tr,),
        in_specs=[pl.BlockSpec((1, 1, tr), lambda i: (i, 0, 0), memory_space=pltpu.SMEM),
                  pl.BlockSpec(memory_space=pl.ANY)],
        out_specs=pl.BlockSpec((tr, d), lambda i: (i, 0)),
        out_shape=jax.ShapeDtypeStruct((n_rows, d), h.dtype),
        scratch_shapes=[pltpu.SemaphoreType.DMA(())],
        compiler_params=_cparams(("arbitrary",)),
        name="moe_gather",
    )(src_tok.reshape(n_rows // tr, 1, tr), h)


def _grouped_ffn_kernel(te_ref, tv_ref, xs_ref, wg_ref, wu_ref, wd_ref, o_ref, hb_ref):
    i, f = pl.program_id(0), pl.program_id(1)

    @pl.when(f == 0)
    def _():
        hb_ref[...] = xs_ref[...].astype(BF16)
        o_ref[...] = jnp.zeros_like(o_ref)

    @pl.when(tv_ref[i] > 0)
    def _():
        o_ref[...] += _swiglu_chunk(hb_ref[...], wg_ref.at[0], wu_ref.at[0], wd_ref.at[0])


def _moe_grouped_ffn(xs, tile_expert, tile_valid, w_gate, w_up, w_down):
    n_rows, d = xs.shape
    d_ff = w_gate.shape[2]
    tm, tf = TM_FFN, TF_FFN
    nf = d_ff // tf

    def f_idx(i, f, te, tv):
        return jnp.where(tv[i] > 0, f, nf - 1)

    return pl.pallas_call(
        _grouped_ffn_kernel,
        grid_spec=pltpu.PrefetchScalarGridSpec(
            num_scalar_prefetch=2,
            grid=(n_rows // tm, nf),
            in_specs=[pl.BlockSpec((tm, d), lambda i, f, te, tv: (i, 0)),
                      pl.BlockSpec((1, d, tf), lambda i, f, te, tv: (te[i], 0, f_idx(i, f, te, tv))),
                      pl.BlockSpec((1, d, tf), lambda i, f, te, tv: (te[i], 0, f_idx(i, f, te, tv))),
                      pl.BlockSpec((1, tf, d), lambda i, f, te, tv: (te[i], f_idx(i, f, te, tv), 0))],
            out_specs=pl.BlockSpec((tm, d), lambda i, f, te, tv: (i, 0)),
            scratch_shapes=[pltpu.VMEM((tm, d), BF16)]),
        out_shape=jax.ShapeDtypeStruct((n_rows, d), F32),
        compiler_params=_cparams(("arbitrary", "arbitrary")),
        name="moe_grouped_swiglu",
    )(tile_expert, tile_valid, xs, w_gate.astype(BF16), w_up.astype(BF16), w_down.astype(BF16))


def _combine_kernel(pos_ref, ys_hbm, x_ref, rt_ref, o_ref, buf_ref, sem):
    tc = x_ref.shape[0]
    _gather_rows(pos_ref, ys_hbm, buf_ref, sem, 2 * tc)
    rt = rt_ref[...]
    o_ref[...] = (x_ref[...] + rt[:, 2:3] * buf_ref[0:tc, :] + rt[:, 3:4] * buf_ref[tc:2 * tc, :])


def _moe_combine(ys, pos, x, route):
    t, d = x.shape
    tc = TC_COMBINE
    return pl.pallas_call(
        _combine_kernel,
        grid=(t // tc,),
        in_specs=[pl.BlockSpec((1, 1, 2 * tc), lambda i: (i, 0, 0), memory_space=pltpu.SMEM),
                  pl.BlockSpec(memory_space=pl.ANY),
                  pl.BlockSpec((tc, d), lambda i: (i, 0)),
                  pl.BlockSpec((tc, LANES), lambda i: (i, 0))],
        out_specs=pl.BlockSpec((tc, d), lambda i: (i, 0)),
        out_shape=jax.ShapeDtypeStruct((t, d), F32),
        scratch_shapes=[pltpu.VMEM((2 * tc, d), F32), pltpu.SemaphoreType.DMA(())],
        compiler_params=_cparams(("arbitrary",)),
        name="moe_combine",
    )(pos, ys, x, route)


def _moe(h, x, route, w_gate, w_up, w_down):
    t = x.shape[0]
    tm, tc = TM_FFN, TC_COMBINE
    n_assign = TOP_K * t
    n_rows = n_assign + N_EXPERTS * tm
    expert = route[:, 0:TOP_K].astype(jnp.int32).reshape(-1)
    onehot = (expert[:, None] == jnp.arange(N_EXPERTS, dtype=jnp.int32)[None, :]).astype(jnp.int32)
    rank = jnp.sum((jnp.cumsum(onehot, axis=0) - onehot) * onehot, axis=1)
    counts = jnp.sum(onehot, axis=0)
    padded = ((counts + tm - 1) // tm) * tm
    group_end = jnp.cumsum(padded)
    dest = (group_end - padded)[expert] + rank
    src_tok = jnp.zeros((n_rows,), jnp.int32).at[dest].set(jnp.arange(n_assign, dtype=jnp.int32) // TOP_K)
    tile_first = jnp.arange(n_rows // tm, dtype=jnp.int32) * tm
    tile_valid = (tile_first < group_end[-1]).astype(jnp.int32)
    tile_expert = jnp.minimum(jnp.searchsorted(group_end, tile_first, side='right'),
                              N_EXPERTS - 1).astype(jnp.int32)
    last_expert = jnp.max(jnp.where(tile_valid > 0, tile_expert, 0))
    tile_expert = jnp.where(tile_valid > 0, tile_expert, last_expert)
    pos = dest.reshape(t // tc, tc, TOP_K).transpose(0, 2, 1).reshape(t // tc, 1, TOP_K * tc)

    xs = _moe_gather(h, src_tok)
    ys = _moe_grouped_ffn(xs, tile_expert, tile_valid, w_gate, w_up, w_down)
    return _moe_combine(ys, pos, x, route)


@jax.jit
def _forward(x, attn_norm, ffn_norm, w_in, b_forget, fox_q_norm, fox_k_norm,
             ssm_a_re, ssm_a_im, ssm_log_dt, ssm_b_re, ssm_b_im, ssm_c_re,
             ssm_c_im, ssm_d, ssm_w_glu, ssm_b_glu, out_norm, w_out,
             dense_w_gate, dense_w_up, dense_w_down, moe_w_router,
             moe_w_gate, moe_w_up, moe_w_down):
    bsz, s_len, d = x.shape
    depth = w_in.shape[0]
    x = x.astype(F32)
    for i in range(depth):
        qa, ka, va, sq, sk, sv, su, c = _inproj(x, attn_norm[i], w_in[i], b_forget[i],
                                                fox_q_norm[i], fox_k_norm[i])
        gain = out_norm[i].astype(F32)
        kv_start = _fox_kv_start(c, fox_q_norm[i], fox_k_norm[i], s_len)
        y_fox = _fox(qa, ka, va, kv_start, gain[:ATTN_WIDTH].reshape(N_PAIRS, 1, LANES))
        y_sb = _sb(sq, sk, sv, gain[ATTN_WIDTH:2 * ATTN_WIDTH].reshape(N_PAIRS, 1, LANES))
        tables = _ssm_tables(ssm_a_re[i], ssm_a_im[i], ssm_log_dt[i], ssm_b_re[i], ssm_b_im[i],
                             ssm_c_re[i], ssm_c_im[i])
        y_ssm = _ssm(su, tables, ssm_d[i], ssm_w_glu[i], ssm_b_glu[i], gain[2 * ATTN_WIDTH:])
        t = bsz * s_len
        flat = (y_fox.reshape(t, -1), y_sb.reshape(t, -1), y_ssm.reshape(t, -1), x.reshape(t, d))
        j = i // 2
        if i % 2 == 0:
            xo, h = _outproj(*flat, w_out[i], ffn_norm[i])
            x = _ffn(h, xo, dense_w_gate[j], dense_w_up[j], dense_w_down[j])
        else:
            xo, h, route = _outproj(*flat, w_out[i], ffn_norm[i], moe_w_router[j])
            x = _moe(h, xo, route, moe_w_gate[j], moe_w_up[j], moe_w_down[j])
        x = x.reshape(bsz, s_len, d)
    return x


def kernel(x, attn_norm, ffn_norm, w_in, b_forget, fox_q_norm, fox_k_norm, ssm_a_re, ssm_a_im,
           ssm_log_dt, ssm_b_re, ssm_b_im, ssm_c_re, ssm_c_im, ssm_d, ssm_w_glu, ssm_b_glu,
           out_norm, w_out, dense_w_gate, dense_w_up, dense_w_down, moe_w_router, moe_w_gate,
           moe_w_up, moe_w_down):
    return _forward(x, attn_norm, ffn_norm, w_in, b_forget, fox_q_norm, fox_k_norm, ssm_a_re,
                    ssm_a_im, ssm_log_dt, ssm_b_re, ssm_b_im, ssm_c_re, ssm_c_im, ssm_d,
                    ssm_w_glu, ssm_b_glu, out_norm, w_out, dense_w_gate, dense_w_up,
                    dense_w_down, moe_w_router, moe_w_gate, moe_w_up, moe_w_down)
```

```python
import functools
import math

import jax
import jax.numpy as jnp
import numpy as np
from jax import lax
from jax.experimental import pallas as pl
from jax.experimental.pallas import tpu as pltpu

F32 = jnp.float32
BF16 = jnp.bfloat16

EPS = 1e-6
LANES = 128
HEAD_DIM = 64
N_HEADS = 6
N_PAIRS = N_HEADS // 2
ATTN_WIDTH = N_HEADS * HEAD_DIM
SSM_GROUP = 16
SSM_GROUPS = 16
SSM_STATE = 64
SSM_WIDTH = SSM_GROUP * SSM_GROUPS
SSM_LANES = SSM_GROUPS * SSM_STATE
N_EXPERTS = 8
TOP_K = 2
QK_SCALE = HEAD_DIM ** -0.5
NEG_BIG = -1e30
LOG_ZERO = -104.0
FOX_MAX_BOUND = 40.0
VMEM_LIMIT = 56 * 1024 * 1024

TM_PROJ = 512
TQ = 512
SB_SUB = 256
SSM_CHUNK = 256
TM_FFN = 1024
TF_FFN = 512
TR_GATHER = 512
TC_COMBINE = 256


def _dot(a, b):
    return jnp.dot(a, b, preferred_element_type=F32)


def _dot_nt(a, b):
    return lax.dot_general(a, b, (((1,), (1,)), ((), ())), preferred_element_type=F32)


def _split2(x):
    hi = x.astype(BF16)
    lo = (x - hi.astype(F32)).astype(BF16)
    return hi, lo


def _split3(x):
    hi = x.astype(BF16)
    r = x - hi.astype(F32)
    mid = r.astype(BF16)
    lo = (r - mid.astype(F32)).astype(BF16)
    return hi, mid, lo


def _softplus(z):
    return jnp.maximum(z, 0.0) + jnp.log1p(jnp.exp(-jnp.abs(z)))


def _cparams(sem):
    return pltpu.CompilerParams(dimension_semantics=sem, vmem_limit_bytes=VMEM_LIMIT)


def _inproj_kernel(x_ref, g_ref, wf_ref, ws_ref, wu_ref, wgh_ref, wgl_ref, bf_ref,
                   ltri_ref, bd_ref, gq_ref, gk_ref, selq_ref, selk_ref, cq_ref, ck_ref,
                   qa_ref, ka_ref, va_ref, sq_ref, sk_ref, sv_ref, su_ref, c_ref,
                   carry_ref):
    tm = x_ref.shape[1]

    @pl.when(pl.program_id(1) == 0)
    def _():
        carry_ref[...] = jnp.zeros_like(carry_ref)

    x = x_ref[0]
    h = x * lax.rsqrt(jnp.mean(x * x, axis=-1, keepdims=True) + EPS) * g_ref[...]
    hb, hl = _split2(h)
    pf = _dot(hb, wf_ref[...])
    ps = _dot(hb, ws_ref[...])
    su_ref[0] = _dot(hb, wu_ref[...])

    fgl = _dot(hb, wgh_ref[...]) + _dot(hb, wgl_ref[...]) + _dot(hl, wgh_ref[...])
    lane = lax.broadcasted_iota(jnp.int32, (tm, LANES), 1)
    lf = jnp.where(lane < N_HEADS, -_softplus(-(fgl + bf_ref[...])), 0.0)
    lt = ltri_ref[...]
    l_hi, l_mid, l_lo = _split3(lf)
    c = _dot(lt, l_hi) + _dot(lt, l_mid) + _dot(lt, l_lo) + carry_ref[...]
    carry_ref[...] = c[tm - 1:tm, :]
    c_ref[0] = c

    cparts = jnp.concatenate(_split3(c), axis=1)
    augq = _dot(cparts, selq_ref[...]) + cq_ref[...]
    augk = _dot(cparts, selk_ref[...]) + ck_ref[...]

    w = ATTN_WIDTH
    fq, fk, fv = pf[:, 0:w], pf[:, w:2 * w], pf[:, 2 * w:3 * w]
    bd = bd_ref[...]
    qn = fq * lax.rsqrt(_dot((fq * fq).astype(BF16), bd) + EPS) * gq_ref[...]
    kn = fk * lax.rsqrt(_dot((fk * fk).astype(BF16), bd) + EPS) * gk_ref[...]
    sq, sk, sv = ps[:, 0:w] * QK_SCALE, ps[:, w:2 * w], ps[:, 2 * w:3 * w]

    for h_idx in range(N_HEADS):
        p, hh = divmod(h_idx, 2)
        in_head = (lane < HEAD_DIM) if hh == 0 else (lane >= HEAD_DIM)
        one_lane = HEAD_DIM if hh == 0 else 0
        ts = slice(LANES * p, LANES * (p + 1))
        ah = slice(LANES * h_idx, LANES * (h_idx + 1))
        qa_ref[0, h_idx] = jnp.where(in_head, qn[:, ts], augq[:, ah]).astype(BF16)
        ka_ref[0, h_idx] = jnp.where(in_head, kn[:, ts], augk[:, ah]).astype(BF16)
        ones_col = jnp.where(lane == one_lane, 1.0, 0.0)
        va_ref[0, h_idx] = jnp.where(in_head, fv[:, ts], ones_col).astype(BF16)
        sq_ref[0, h_idx] = jnp.where(in_head, sq[:, ts], 0.0).astype(BF16)
        sk_ref[0, h_idx] = jnp.where(in_head, sk[:, ts], 0.0).astype(BF16)
        sv_ref[0, h_idx] = jnp.where(in_head, sv[:, ts], 0.0).astype(BF16)


def _aug_tables():
    selq = np.zeros((3 * LANES, N_HEADS * LANES), np.float32)
    selk = np.zeros((3 * LANES, N_HEADS * LANES), np.float32)
    cq = np.zeros((1, N_HEADS * LANES), np.float32)
    ck = np.zeros((1, N_HEADS * LANES), np.float32)
    shift = np.zeros((1, N_HEADS * LANES), np.float32)
    for h in range(N_HEADS):
        base = LANES * h + (HEAD_DIM if h % 2 == 0 else 0)
        for piece in range(3):
            selq[piece * LANES + h, base + piece] = 1.0
            selk[piece * LANES + h, base + 3 + piece] = -1.0
            cq[0, base + 3 + piece] = 1.0
            ck[0, base + piece] = 1.0
        cq[0, base + 6] = 1.0
        shift[0, base + 6] = 1.0
    return selq, selk, cq, ck, shift


def _inproj(x, gain, w_in, b_forget, q_gain, k_gain, bound):
    bsz, s_len, d = x.shape
    tm = TM_PROJ
    w = ATTN_WIDTH
    o_fg = 3 * w
    o_s = o_fg + N_HEADS
    o_u = o_s + 3 * w
    wf = w_in[:, :o_fg].astype(BF16)
    ws = w_in[:, o_s:o_u].astype(BF16)
    wu = w_in[:, o_u:].astype(BF16)
    wg = jnp.pad(w_in[:, o_fg:o_s], ((0, 0), (0, LANES - N_HEADS)))
    wgh, wgl = _split2(wg)
    bfp = jnp.pad(b_forget.astype(F32), (0, LANES - N_HEADS))[None, :]
    ltri = jnp.asarray(np.tril(np.ones((tm, tm), np.float32)), BF16)
    bd = jnp.asarray(np.kron(np.eye(N_HEADS, dtype=np.float32),
                             np.full((HEAD_DIM, HEAD_DIM), 1.0 / HEAD_DIM, np.float32)), BF16)
    gq = (jnp.tile(q_gain.astype(F32), N_HEADS) * QK_SCALE)[None, :]
    gk = jnp.tile(k_gain.astype(F32), N_HEADS)[None, :]
    selq, selk, cq, ck, shift = _aug_tables()
    selq, selk = jnp.asarray(selq, BF16), jnp.asarray(selk, BF16)
    cq = jnp.asarray(cq)
    ck = jnp.asarray(ck) - bound.astype(BF16).astype(F32) * jnp.asarray(shift)

    def full(a):
        return pl.BlockSpec(a.shape, lambda b, i: (0,) * a.ndim)

    consts = (gain.astype(F32)[None, :], wf, ws, wu, wgh, wgl, bfp, ltri, bd, gq, gk,
              selq, selk, cq, ck)
    head_shape = jax.ShapeDtypeStruct((bsz, N_HEADS, s_len, LANES), BF16)
    head_spec = pl.BlockSpec((1, N_HEADS, tm, LANES), lambda b, i: (b, 0, i, 0))
    return pl.pallas_call(
        _inproj_kernel,
        grid=(bsz, s_len // tm),
        in_specs=[pl.BlockSpec((1, tm, d), lambda b, i: (b, i, 0))] + [full(a) for a in consts],
        out_specs=[head_spec] * 6 + [
            pl.BlockSpec((1, tm, SSM_WIDTH), lambda b, i: (b, i, 0)),
            pl.BlockSpec((1, tm, LANES), lambda b, i: (b, i, 0))],
        out_shape=[head_shape] * 6 + [
            jax.ShapeDtypeStruct((bsz, s_len, SSM_WIDTH), F32),
            jax.ShapeDtypeStruct((bsz, s_len, LANES), F32)],
        scratch_shapes=[pltpu.VMEM((1, LANES), F32)],
        compiler_params=_cparams(("arbitrary", "arbitrary")),
        name="inproj",
    )(x, *consts)


def _pair_rms(y, gain):
    lane = lax.broadcasted_iota(jnp.int32, y.shape, 1)
    lo = lane < HEAD_DIM
    y2 = y * y
    ms0 = jnp.sum(jnp.where(lo, y2, 0.0), axis=-1, keepdims=True) * (1.0 / HEAD_DIM)
    ms1 = jnp.sum(jnp.where(lo, 0.0, y2), axis=-1, keepdims=True) * (1.0 / HEAD_DIM)
    return y * jnp.where(lo, lax.rsqrt(ms0 + EPS), lax.rsqrt(ms1 + EPS)) * gain


def _fox_kernel(kvs_ref, qa_ref, ka_ref, va_ref, gain_ref, o_ref, acc_ref, m_ref, *, nq, bounded):
    tq = qa_ref.shape[2]
    b, p, i = pl.program_id(0), pl.program_id(1), pl.program_id(2)
    row = lax.broadcasted_iota(jnp.int32, (tq, tq), 0)
    col = lax.broadcasted_iota(jnp.int32, (tq, tq), 1)

    for hh in range(2):
        q = qa_ref[0, hh]
        start = kvs_ref[(b * N_HEADS + 2 * p + hh) * nq + i]
        acc_ref[hh] = jnp.zeros((tq, LANES), F32)
        if not bounded:
            m_ref[hh] = jnp.full((tq, 1), NEG_BIG, F32)

        def step(j, masked, hh=hh, q=q):
            off = pl.multiple_of(j * tq, tq)
            k = ka_ref[0, hh, pl.ds(off, tq), :]
            v = va_ref[0, hh, pl.ds(off, tq), :]
            s = _dot_nt(q, k)
            if masked:
                s = jnp.where(col <= row, s, NEG_BIG)
            if bounded:
                acc_ref[hh] += _dot(jnp.exp(s).astype(BF16), v)
                return
            m_prev = m_ref[hh]
            m_new = jnp.maximum(m_prev, jnp.max(s, axis=-1, keepdims=True))
            pexp = jnp.exp(s - m_new)
            acc_ref[hh] = jnp.exp(m_prev - m_new) * acc_ref[hh] + _dot(pexp.astype(BF16), v)
            m_ref[hh] = m_new

        def body(j, carry):
            step(j, False)
            return carry

        lax.fori_loop(start, i, body, 0)
        step(i, True)

    lane = lax.broadcasted_iota(jnp.int32, (tq, LANES), 1)
    a0, a1 = acc_ref[0], acc_ref[1]
    y = jnp.where(lane < HEAD_DIM, a0 / a0[:, HEAD_DIM:HEAD_DIM + 1], a1 / a1[:, 0:1])
    o_ref[0] = _pair_rms(y, gain_ref[0]).astype(BF16)


def _fox_call(kv_start, qa, ka, va, gain3, *, bounded):
    bsz, _, s_len, _ = qa.shape
    tq = TQ
    nq = s_len // tq
    kv_spec = pl.BlockSpec((1, 2, s_len, LANES), lambda b, p, i, kvs: (b, p, 0, 0))
    return pl.pallas_call(
        functools.partial(_fox_kernel, nq=nq, bounded=bounded),
        grid_spec=pltpu.PrefetchScalarGridSpec(
            num_scalar_prefetch=1,
            grid=(bsz, N_PAIRS, nq),
            in_specs=[pl.BlockSpec((1, 2, tq, LANES), lambda b, p, i, kvs: (b, p, i, 0)),
                      kv_spec, kv_spec,
                      pl.BlockSpec((1, 1, LANES), lambda b, p, i, kvs: (p, 0, 0))],
            out_specs=pl.BlockSpec((1, tq, LANES), lambda b, p, i, kvs: (b, i, p)),
            scratch_shapes=[pltpu.VMEM((2, tq, LANES), F32), pltpu.VMEM((2, tq, 1), F32)]),
        out_shape=jax.ShapeDtypeStruct((bsz, s_len, ATTN_WIDTH), BF16),
        compiler_params=_cparams(("arbitrary", "arbitrary", "arbitrary")),
        name="fox_attention_bounded" if bounded else "fox_attention",
    )(kv_start, qa, ka, va, gain3)


def _fox(qa, ka, va, kv_start, gain3, bound):
    return lax.cond(bound <= FOX_MAX_BOUND,
                    functools.partial(_fox_call, bounded=True),
                    functools.partial(_fox_call, bounded=False),
                    kv_start, qa, ka, va, gain3)


def _fox_bound(q_gain, k_gain):
    return 1.05 * HEAD_DIM * QK_SCALE * jnp.max(jnp.abs(q_gain)) * jnp.max(jnp.abs(k_gain))


def _fox_kv_start(c, bound, s_len):
    tq = TQ
    nq = s_len // tq
    ch = c[:, :, :N_HEADS]
    c_first = ch[:, 0::tq, :]
    c_end = ch[:, tq - 1::tq, :]
    skip = (c_first[:, :, None, :] - c_end[:, None, :, :]) < (LOG_ZERO - 2.0 * bound)
    start = jnp.sum(skip.astype(jnp.int32), axis=2)
    start = jnp.minimum(start, jnp.arange(nq, dtype=jnp.int32)[None, :, None])
    return start.transpose(0, 2, 1).reshape(-1)


def _sb_kernel(q_ref, k_ref, v_ref, u_ref, gain_ref, o_ref, acc_ref, r_ref):
    tq = q_ref.shape[2]
    i = pl.program_id(2)
    row = lax.broadcasted_iota(jnp.int32, (tq, tq), 0)
    col = lax.broadcasted_iota(jnp.int32, (tq, tq), 1)

    for hh in range(2):
        q = q_ref[0, hh]
        acc_ref[hh] = jnp.zeros((tq, LANES), F32)
        r_ref[hh] = jnp.zeros((tq, 1), F32)

        def step(j, masked, hh=hh, q=q):
            off = pl.multiple_of(j * tq, tq)
            k = k_ref[0, hh, pl.ds(off, tq), :]
            v = v_ref[0, hh, pl.ds(off, tq), :]
            z = _dot_nt(q, k)
            sp = jnp.maximum(z, 0.0) + jnp.log(1.0 + jnp.exp(-jnp.abs(z)))
            log_not = jnp.where(col < row, -sp, 0.0) if masked else -sp
            u = u_ref[...]
            pieces = []
            tail = jnp.zeros((tq, 1), F32)
            for c0 in range(tq - SB_SUB, -1, -SB_SUB):
                part = log_not[:, c0:c0 + SB_SUB]
                n_hi, n_lo = _split2(part)
                later_part = _dot(n_hi, u) + _dot(n_lo, u) + tail
                tail = later_part[:, 0:1] + part[:, 0:1]
                pieces.insert(0, later_part)
            later_in = jnp.concatenate(pieces, axis=1)
            r_prev = r_ref[hh]
            a = jnp.exp(z - sp + later_in + r_prev)
            if masked:
                a = jnp.where(col < row, a, 0.0)
            acc_ref[hh] += _dot(a.astype(BF16), v)
            r_new = r_prev + tail
            r_ref[hh] = r_new
            return jnp.max(r_new)

        def cond(carry):
            j, r_max = carry
            return jnp.logical_and(j >= 0, r_max >= LOG_ZERO)

        def body(carry):
            j, _ = carry
            return j - 1, step(j, False)

        lax.while_loop(cond, body, (i - 1, step(i, True)))

    lane = lax.broadcasted_iota(jnp.int32, (tq, LANES), 1)
    y = jnp.where(lane < HEAD_DIM, acc_ref[0], acc_ref[1])
    o_ref[0] = _pair_rms(y, gain_ref[0]).astype(BF16)


def _sb(sq, sk, sv, gain3):
    bsz, _, s_len, _ = sq.shape
    tq = TQ
    upper = jnp.asarray(np.tril(np.ones((SB_SUB, SB_SUB), np.float32), -1), BF16)
    kv_spec = pl.BlockSpec((1, 2, s_len, LANES), lambda b, p, i: (b, p, 0, 0))
    return pl.pallas_call(
        _sb_kernel,
        grid=(bsz, N_PAIRS, s_len // tq),
        in_specs=[pl.BlockSpec((1, 2, tq, LANES), lambda b, p, i: (b, p, i, 0)),
                  kv_spec, kv_spec,
                  pl.BlockSpec((SB_SUB, SB_SUB), lambda b, p, i: (0, 0)),
                  pl.BlockSpec((1, 1, LANES), lambda b, p, i: (p, 0, 0))],
        out_specs=pl.BlockSpec((1, tq, LANES), lambda b, p, i: (b, i, p)),
        out_shape=jax.ShapeDtypeStruct((bsz, s_len, ATTN_WIDTH), BF16),
        scratch_shapes=[pltpu.VMEM((2, tq, LANES), F32), pltpu.VMEM((2, tq, 1), F32)],
        compiler_params=_cparams(("arbitrary", "arbitrary", "arbitrary")),
        name="stickbreak_attention",
    )(sq, sk, sv, upper, gain3)


def _ssm_kernel(u_ref, bre_ref, bim_ref, ccat_ref, pw_ref, d_ref, wglu_ref, bglu_ref,
                bd_ref, gain_ref, o_ref, xr_ref, xi_ref, hc_ref):
    chunk = u_ref.shape[1]

    @pl.when(pl.program_id(1) == 0)
    def _():
        hc_ref[...] = jnp.zeros_like(hc_ref)

    u = u_ref[0]
    ub = u.astype(BF16)
    xr_ref[...] = _dot(ub, bre_ref[...])
    xi_ref[...] = _dot(ub, bim_ref[...])

    def group(g, carry):
        hr, hi = carry
        rows = pl.ds(pl.multiple_of(g * 8, 8), 8)
        xr, xi = xr_ref[rows, :], xi_ref[rows, :]
        for k, d in enumerate((1, 2, 4)):
            mr, mi = pw_ref[2 + 2 * k], pw_ref[3 + 2 * k]
            sr, si = pltpu.roll(xr, d, 0), pltpu.roll(xi, d, 0)
            xr, xi = xr + mr * sr - mi * si, xi + mr * si + mi * sr
        pr, pi = pw_ref[0], pw_ref[1]
        xr, xi = xr + pr * hr - pi * hi, xi + pr * hi + pi * hr
        xr_ref[rows, :] = xr
        xi_ref[rows, :] = xi
        return xr[7:8, :], xi[7:8, :]

    hr, hi = lax.fori_loop(0, chunk // 8, group, (hc_ref[0:1, :], hc_ref[1:2, :]))
    hc_ref[0:1, :] = hr
    hc_ref[1:2, :] = hi

    hcat = jnp.concatenate([xr_ref[...], xi_ref[...]], axis=1).astype(BF16)
    y = _dot(hcat, ccat_ref[...]) + d_ref[...] * u
    g = 0.5 * y * (1.0 + jnp.tanh(math.sqrt(2.0 / math.pi) * (y + 0.044715 * (y * y * y))))
    g = g * jax.nn.sigmoid(_dot(g.astype(BF16), wglu_ref[...]) + bglu_ref[...])
    ms = _dot((g * g).astype(BF16), bd_ref[...])
    o_ref[0] = (g * lax.rsqrt(ms + EPS) * gain_ref[...]).astype(BF16)


def _ssm_tables(a_re, a_im, log_dt, b_re, b_im, c_re, c_im):
    a_re = a_re.astype(F32)
    a_im = a_im.astype(F32)
    dt = jnp.exp(log_dt.astype(F32))[:, None]
    mag = jnp.exp(dt * a_re)
    ang = dt * a_im
    abar_re = mag * jnp.cos(ang)
    abar_im = mag * jnp.sin(ang)
    num_re = abar_re - 1.0
    num_im = abar_im
    den = a_re * a_re + a_im * a_im
    coef_re = (num_re * a_re + num_im * a_im) / den
    coef_im = (num_im * a_re - num_re * a_im) / den
    b_re = b_re.astype(F32)
    b_im = b_im.astype(F32)
    bbar_re = coef_re[..., None] * b_re - coef_im[..., None] * b_im
    bbar_im = coef_re[..., None] * b_im + coef_im[..., None] * b_re
    eye = jnp.eye(SSM_GROUPS, dtype=F32)

    def b_blockdiag(bb):
        return jnp.einsum('gnc,gh->gchn', bb, eye).reshape(SSM_WIDTH, SSM_LANES).astype(BF16)

    def c_blockdiag(cc):
        return jnp.einsum('gcn,gh->gnhc', cc.astype(F32), eye).reshape(SSM_LANES, SSM_WIDTH)

    ccat = jnp.concatenate([c_blockdiag(c_re), -c_blockdiag(c_im)], axis=0).astype(BF16)

    ar = abar_re.reshape(1, SSM_LANES)
    ai = abar_im.reshape(1, SSM_LANES)
    pows = [(ar, ai)]
    for _ in range(7):
        pr, pi = pows[-1]
        pows.append((pr * ar - pi * ai, pr * ai + pi * ar))
    p_re = jnp.concatenate([p[0] for p in pows], axis=0)
    p_im = jnp.concatenate([p[1] for p in pows], axis=0)
    rows = jnp.arange(8)[:, None]
    tabs = [p_re, p_im]
    for d in (1, 2, 4):
        tabs.append(jnp.where(rows >= d, pows[d - 1][0], 0.0))
        tabs.append(jnp.where(rows >= d, pows[d - 1][1], 0.0))
    return b_blockdiag(bbar_re), b_blockdiag(bbar_im), ccat, jnp.stack(tabs, axis=0)


def _ssm(su, tables, d_skip, w_glu, b_glu, gain):
    bsz, s_len, _ = su.shape
    chunk = SSM_CHUNK
    bre, bim, ccat, pw = tables
    bd = jnp.asarray(np.kron(np.eye(SSM_GROUPS, dtype=np.float32),
                             np.full((SSM_GROUP, SSM_GROUP), 1.0 / SSM_GROUP, np.float32)), BF16)
    consts = (bre, bim, ccat, pw, d_skip.astype(F32)[None, :], w_glu.astype(BF16),
              b_glu.astype(F32)[None, :], bd, gain.astype(F32)[None, :])

    def full(a):
        return pl.BlockSpec(a.shape, lambda b, i: (0,) * a.ndim)

    return pl.pallas_call(
        _ssm_kernel,
        grid=(bsz, s_len // chunk),
        in_specs=[pl.BlockSpec((1, chunk, SSM_WIDTH), lambda b, i: (b, i, 0))]
        + [full(a) for a in consts],
        out_specs=pl.BlockSpec((1, chunk, SSM_WIDTH), lambda b, i: (b, i, 0)),
        out_shape=jax.ShapeDtypeStruct((bsz, s_len, SSM_WIDTH), BF16),
        scratch_shapes=[pltpu.VMEM((chunk, SSM_LANES), F32), pltpu.VMEM((chunk, SSM_LANES), F32),
                        pltpu.VMEM((2, SSM_LANES), F32)],
        compiler_params=_cparams(("arbitrary", "arbitrary")),
        name="s5_mixer",
    )(su, *consts)


def _outproj_kernel(*refs, routed):
    if routed:
        yf_ref, ys_ref, ym_ref, x_ref, wo_ref, g_ref, wrh_ref, wrl_ref, xo_ref, h_ref, rt_ref = refs
    else:
        yf_ref, ys_ref, ym_ref, x_ref, wo_ref, g_ref, xo_ref, h_ref = refs
    mix = jnp.concatenate([yf_ref[...], ys_ref[...], ym_ref[...]], axis=1)
    x = x_ref[...] + _dot(mix, wo_ref[...])
    xo_ref[...] = x
    h = x * lax.rsqrt(jnp.mean(x * x, axis=-1, keepdims=True) + EPS) * g_ref[...]
    if not routed:
        h_ref[...] = h.astype(BF16)
        return
    h_ref[...] = h
    hb, hl = _split2(h)
    logits = _dot(hb, wrh_ref[...]) + _dot(hb, wrl_ref[...]) + _dot(hl, wrh_ref[...])
    lane = lax.broadcasted_iota(jnp.int32, logits.shape, 1).astype(F32)
    logits = jnp.where(lane < N_EXPERTS, logits, NEG_BIG)
    m1 = jnp.max(logits, axis=-1, keepdims=True)
    i1 = jnp.min(jnp.where(logits == m1, lane, float(LANES)), axis=-1, keepdims=True)
    rest = jnp.where(lane == i1, NEG_BIG, logits)
    m2 = jnp.max(rest, axis=-1, keepdims=True)
    i2 = jnp.min(jnp.where(rest == m2, lane, float(LANES)), axis=-1, keepdims=True)
    e2 = jnp.exp(m2 - m1)
    w1 = 1.0 / (1.0 + e2)
    w2 = e2 / (1.0 + e2)
    rt_ref[...] = jnp.where(lane == 0, i1, jnp.where(lane == 1, i2, jnp.where(
        lane == 2, w1, jnp.where(lane == 3, w2, 0.0))))


def _outproj(yf, ys, ym, x, w_out, ffn_gain, w_router=None):
    t, d = x.shape
    tm = TM_PROJ
    routed = w_router is not None
    consts = [w_out.astype(BF16), ffn_gain.astype(F32)[None, :]]
    if routed:
        consts += list(_split2(jnp.pad(w_router.astype(F32), ((0, 0), (0, LANES - N_EXPERTS)))))

    def rows(width):
        return pl.BlockSpec((tm, width), lambda i: (i, 0))

    def full(a):
        return pl.BlockSpec(a.shape, lambda i: (0,) * a.ndim)

    out_specs = [rows(d), rows(d)]
    out_shape = [jax.ShapeDtypeStruct((t, d), F32), jax.ShapeDtypeStruct((t, d), F32 if routed else BF16)]
    if routed:
        out_specs.append(rows(LANES))
        out_shape.append(jax.ShapeDtypeStruct((t, LANES), F32))
    return pl.pallas_call(
        functools.partial(_outproj_kernel, routed=routed),
        grid=(t // tm,),
        in_specs=[rows(ATTN_WIDTH), rows(ATTN_WIDTH), rows(SSM_WIDTH), rows(d)]
        + [full(a) for a in consts],
        out_specs=out_specs,
        out_shape=out_shape,
        compiler_params=_cparams(("arbitrary",)),
        name="outproj_routed" if routed else "outproj",
    )(yf, ys, ym, x, *consts)


def _swiglu_chunk(h, wg_ref, wu_ref, wd_ref):
    g = _dot(h, wg_ref[...])
    u = _dot(h, wu_ref[...])
    return _dot((g * jax.nn.sigmoid(g) * u).astype(BF16), wd_ref[...])


def _ffn_kernel(h_ref, x_ref, wg_ref, wu_ref, wd_ref, o_ref):
    @pl.when(pl.program_id(1) == 0)
    def _():
        o_ref[...] = x_ref[...]

    o_ref[...] += _swiglu_chunk(h_ref[...], wg_ref, wu_ref, wd_ref)


def _ffn(h, x, w_gate, w_up, w_down):
    t, d = x.shape
    d_ff = w_gate.shape[1]
    tm, tf = TM_FFN, TF_FFN
    return pl.pallas_call(
        _ffn_kernel,
        grid=(t // tm, d_ff // tf),
        in_specs=[pl.BlockSpec((tm, d), lambda i, f: (i, 0)),
                  pl.BlockSpec((tm, d), lambda i, f: (i, 0)),
                  pl.BlockSpec((d, tf), lambda i, f: (0, f)),
                  pl.BlockSpec((d, tf), lambda i, f: (0, f)),
                  pl.BlockSpec((tf, d), lambda i, f: (f, 0))],
        out_specs=pl.BlockSpec((tm, d), lambda i, f: (i, 0)),
        out_shape=jax.ShapeDtypeStruct((t, d), F32),
        compiler_params=_cparams(("arbitrary", "arbitrary")),
        name="dense_swiglu",
    )(h, x, w_gate.astype(BF16), w_up.astype(BF16), w_down.astype(BF16))


def _row_copy(src_hbm, dst_ref, src_row, dst_row, sem):
    return pltpu.make_async_copy(src_hbm.at[pl.ds(src_row, 1), :],
                                 dst_ref.at[pl.ds(dst_row, 1), :], sem)


def _gather_rows(idx_ref, src_hbm, dst_ref, sem, n_rows):
    def start(r, carry):
        _row_copy(src_hbm, dst_ref, idx_ref[0, 0, r], r, sem).start()
        return carry

    def wait(r, carry):
        _row_copy(src_hbm, dst_ref, idx_ref[0, 0, r], r, sem).wait()
        return carry

    lax.fori_loop(0, n_rows, start, 0)
    lax.fori_loop(0, n_rows, wait, 0)


def _gather_kernel(idx_ref, h_hbm, o_ref, sem):
    _gather_rows(idx_ref, h_hbm, o_ref, sem, o_ref.shape[0])


def _moe_gather(h, src_tok):
    n_rows = src_tok.shape[0]
    d = h.shape[1]
    tr = TR_GATHER
    return pl.pallas_call(
        _gather_kernel,
        grid=(n_rows // tr,),
        in_specs=[pl.BlockSpec((1, 1, tr), lambda i: (i, 0, 0), memory_space=pltpu.SMEM),
                  pl.BlockSpec(memory_space=pl.ANY)],
        out_specs=pl.BlockSpec((tr, d), lambda i: (i, 0)),
        out_shape=jax.ShapeDtypeStruct((n_rows, d), h.dtype),
        scratch_shapes=[pltpu.SemaphoreType.DMA(())],
        compiler_params=_cparams(("arbitrary",)),
        name="moe_gather",
    )(src_tok.reshape(n_rows // tr, 1, tr), h)


def _grouped_ffn_kernel(te_ref, tv_ref, xs_ref, wg_ref, wu_ref, wd_ref, o_ref, hb_ref):
    i, f = pl.program_id(0), pl.program_id(1)

    @pl.when(f == 0)
    def _():
        hb_ref[...] = xs_ref[...].astype(BF16)
        o_ref[...] = jnp.zeros_like(o_ref)

    @pl.when(tv_ref[i] > 0)
    def _():
        o_ref[...] += _swiglu_chunk(hb_ref[...], wg_ref.at[0], wu_ref.at[0], wd_ref.at[0])


def _moe_grouped_ffn(xs, tile_expert, tile_valid, w_gate, w_up, w_down):
    n_rows, d = xs.shape
    d_ff = w_gate.shape[2]
    tm, tf = TM_FFN, TF_FFN
    nf = d_ff // tf

    def f_idx(i, f, te, tv):
        return jnp.where(tv[i] > 0, f, nf - 1)

    return pl.pallas_call(
        _grouped_ffn_kernel,
        grid_spec=pltpu.PrefetchScalarGridSpec(
            num_scalar_prefetch=2,
            grid=(n_rows // tm, nf),
            in_specs=[pl.BlockSpec((tm, d), lambda i, f, te, tv: (i, 0)),
                      pl.BlockSpec((1, d, tf), lambda i, f, te, tv: (te[i], 0, f_idx(i, f, te, tv))),
                      pl.BlockSpec((1, d, tf), lambda i, f, te, tv: (te[i], 0, f_idx(i, f, te, tv))),
                      pl.BlockSpec((1, tf, d), lambda i, f, te, tv: (te[i], f_idx(i, f, te, tv), 0))],
            out_specs=pl.BlockSpec((tm, d), lambda i, f, te, tv: (i, 0)),
            scratch_shapes=[pltpu.VMEM((tm, d), BF16)]),
        out_shape=jax.ShapeDtypeStruct((n_rows, d), F32),
        compiler_params=_cparams(("arbitrary", "arbitrary")),
        name="moe_grouped_swiglu",
    )(tile_expert, tile_valid, xs, w_gate.astype(BF16), w_up.astype(BF16), w_down.astype(BF16))


def _combine_kernel(pos_ref, ys_hbm, x_ref, rt_ref, o_ref, buf_ref, sem):
    tc = x_ref.shape[0]
    _gather_rows(pos_ref, ys_hbm, buf_ref, sem, 2 * tc)
    rt = rt_ref[...]
    o_ref[...] = (x_ref[...] + rt[:, 2:3] * buf_ref[0:tc, :] + rt[:, 3:4] * buf_ref[tc:2 * tc, :])


def _moe_combine(ys, pos, x, route):
    t, d = x.shape
    tc = TC_COMBINE
    return pl.pallas_call(
        _combine_kernel,
        grid=(t // tc,),
        in_specs=[pl.BlockSpec((1, 1, 2 * tc), lambda i: (i, 0, 0), memory_space=pltpu.SMEM),
                  pl.BlockSpec(memory_space=pl.ANY),
                  pl.BlockSpec((tc, d), lambda i: (i, 0)),
                  pl.BlockSpec((tc, LANES), lambda i: (i, 0))],
        out_specs=pl.BlockSpec((tc, d), lambda i: (i, 0)),
        out_shape=jax.ShapeDtypeStruct((t, d), F32),
        scratch_shapes=[pltpu.VMEM((2 * tc, d), F32), pltpu.SemaphoreType.DMA(())],
        compiler_params=_cparams(("arbitrary",)),
        name="moe_combine",
    )(pos, ys, x, route)


def _moe(h, x, route, w_gate, w_up, w_down):
    t = x.shape[0]
    tm, tc = TM_FFN, TC_COMBINE
    n_assign = TOP_K * t
    n_rows = n_assign + N_EXPERTS * tm
    expert = route[:, 0:TOP_K].astype(jnp.int32).reshape(-1)
    onehot = (expert[:, None] == jnp.arange(N_EXPERTS, dtype=jnp.int32)[None, :]).astype(jnp.int32)
    rank = jnp.sum((jnp.cumsum(onehot, axis=0) - onehot) * onehot, axis=1)
    counts = jnp.sum(onehot, axis=0)
    padded = ((counts + tm - 1) // tm) * tm
    group_end = jnp.cumsum(padded)
    dest = (group_end - padded)[expert] + rank
    src_tok = jnp.zeros((n_rows,), jnp.int32).at[dest].set(jnp.arange(n_assign, dtype=jnp.int32) // TOP_K)
    tile_first = jnp.arange(n_rows // tm, dtype=jnp.int32) * tm
    tile_valid = (tile_first < group_end[-1]).astype(jnp.int32)
    tile_expert = jnp.minimum(jnp.searchsorted(group_end, tile_first, side='right'),
                              N_EXPERTS - 1).astype(jnp.int32)
    last_expert = jnp.max(jnp.where(tile_valid > 0, tile_expert, 0))
    tile_expert = jnp.where(tile_valid > 0, tile_expert, last_expert)
    pos = dest.reshape(t // tc, tc, TOP_K).transpose(0, 2, 1).reshape(t // tc, 1, TOP_K * tc)

    xs = _moe_gather(h, src_tok)
    ys = _moe_grouped_ffn(xs, tile_expert, tile_valid, w_gate, w_up, w_down)
    return _moe_combine(ys, pos, x, route)


@jax.jit
def _forward(x, attn_norm, ffn_norm, w_in, b_forget, fox_q_norm, fox_k_norm,
             ssm_a_re, ssm_a_im, ssm_log_dt, ssm_b_re, ssm_b_im, ssm_c_re,
             ssm_c_im, ssm_d, ssm_w_glu, ssm_b_glu, out_norm, w_out,
             dense_w_gate, dense_w_up, dense_w_down, moe_w_router,
             moe_w_gate, moe_w_up, moe_w_down):
    bsz, s_len, d = x.shape
    depth = w_in.shape[0]
    x = x.astype(F32)
    for i in range(depth):
        bound = _fox_bound(fox_q_norm[i], fox_k_norm[i])
        qa, ka, va, sq, sk, sv, su, c = _inproj(x, attn_norm[i], w_in[i], b_forget[i],
                                                fox_q_norm[i], fox_k_norm[i], bound)
        gain = out_norm[i].astype(F32)
        kv_start = _fox_kv_start(c, bound, s_len)
        y_fox = _fox(qa, ka, va, kv_start, gain[:ATTN_WIDTH].reshape(N_PAIRS, 1, LANES), bound)
        y_sb = _sb(sq, sk, sv, gain[ATTN_WIDTH:2 * ATTN_WIDTH].reshape(N_PAIRS, 1, LANES))
        tables = _ssm_tables(ssm_a_re[i], ssm_a_im[i], ssm_log_dt[i], ssm_b_re[i], ssm_b_im[i],
                             ssm_c_re[i], ssm_c_im[i])
        y_ssm = _ssm(su, tables, ssm_d[i], ssm_w_glu[i], ssm_b_glu[i], gain[2 * ATTN_WIDTH:])
        t = bsz * s_len
        flat = (y_fox.reshape(t, -1), y_sb.reshape(t, -1), y_ssm.reshape(t, -1), x.reshape(t, d))
        j = i // 2
        if i % 2 == 0:
            xo, h = _outproj(*flat, w_out[i], ffn_norm[i])
            x = _ffn(h, xo, dense_w_gate[j], dense_w_up[j], dense_w_down[j])
        else:
            xo, h, route = _outproj(*flat, w_out[i], ffn_norm[i], moe_w_router[j])
            x = _moe(h, xo, route, moe_w_gate[j], moe_w_up[j], moe_w_down[j])
        x = x.reshape(bsz, s_len, d)
    return x


def kernel(x, attn_norm, ffn_norm, w_in, b_forget, fox_q_norm, fox_k_norm, ssm_a_re, ssm_a_im,
           ssm_log_dt, ssm_b_re, ssm_b_im, ssm_c_re, ssm_c_im, ssm_d, ssm_w_glu, ssm_b_glu,
           out_norm, w_out, dense_w_gate, dense_w_up, dense_w_down, moe_w_router, moe_w_gate,
           moe_w_up, moe_w_down):
    return _forward(x, attn_norm, ffn_norm, w_in, b_forget, fox_q_norm, fox_k_norm, ssm_a_re,
                    ssm_a_im, ssm_log_dt, ssm_b_re, ssm_b_im, ssm_c_re, ssm_c_im, ssm_d,
                    ssm_w_glu, ssm_b_glu, out_norm, w_out, dense_w_gate, dense_w_up,
                    dense_w_down, moe_w_router, moe_w_gate, moe_w_up, moe_w_down)
```

```python
import functools
import math

import jax
import jax.numpy as jnp
import numpy as np
from jax import lax
from jax.experimental import pallas as pl
from jax.experimental.pallas import tpu as pltpu

F32 = jnp.float32
BF16 = jnp.bfloat16

EPS = 1e-6
LANES = 128
SUBLANES = 8
HEAD_DIM = 64
N_HEADS = 6
N_PAIRS = N_HEADS // 2
ATTN_WIDTH = N_HEADS * HEAD_DIM
SSM_GROUP = 16
SSM_GROUPS = 16
SSM_STATE = 64
SSM_WIDTH = SSM_GROUP * SSM_GROUPS
SSM_LANES = SSM_GROUPS * SSM_STATE
N_EXPERTS = 8
TOP_K = 2
QK_SCALE = HEAD_DIM ** -0.5
NEG_BIG = -1e30
LOG_ZERO = -104.0
FOX_MAX_BOUND = 40.0
VMEM_LIMIT = 56 * 1024 * 1024

TM_PROJ = 512
TQ = 512
TQ_SB = 256
SB_SUB = 256
SSM_CHUNK = 256
TM_FFN = 1024
TF_FFN = 512
TC_COMBINE = 256


def _dot(a, b):
    return jnp.dot(a, b, preferred_element_type=F32)


def _dot_nt(a, b):
    return lax.dot_general(a, b, (((1,), (1,)), ((), ())), preferred_element_type=F32)


def _split2(x):
    hi = x.astype(BF16)
    lo = (x - hi.astype(F32)).astype(BF16)
    return hi, lo


def _split3(x):
    hi = x.astype(BF16)
    r = x - hi.astype(F32)
    mid = r.astype(BF16)
    lo = (r - mid.astype(F32)).astype(BF16)
    return hi, mid, lo


def _softplus(z):
    return jnp.maximum(z, 0.0) + jnp.log1p(jnp.exp(-jnp.abs(z)))


def _store_token_tiles(ref, x, base=0):
    n = x.shape[0]
    for s in range(SUBLANES):
        ref[pl.ds(base + s, n, stride=SUBLANES), :] = x[:, LANES * s:LANES * (s + 1)]


def _load_token_tiles(ref, n, base=0):
    return [ref[pl.ds(base + s, n, stride=SUBLANES), :] for s in range(SUBLANES)]


def _cparams(sem):
    return pltpu.CompilerParams(dimension_semantics=sem, vmem_limit_bytes=VMEM_LIMIT)


def _inproj_kernel(x_ref, g_ref, wf_ref, ws_ref, wu_ref, wgh_ref, wgl_ref, bf_ref,
                   ltri_ref, bd_ref, gq_ref, gk_ref, selq_ref, selk_ref, cq_ref, ck_ref,
                   qa_ref, ka_ref, va_ref, sq_ref, sk_ref, sv_ref, su_ref, c_ref,
                   carry_ref):
    tm = x_ref.shape[1]

    @pl.when(pl.program_id(1) == 0)
    def _():
        carry_ref[...] = jnp.zeros_like(carry_ref)

    x = x_ref[0]
    h = x * lax.rsqrt(jnp.mean(x * x, axis=-1, keepdims=True) + EPS) * g_ref[...]
    hb, hl = _split2(h)
    pf = _dot(hb, wf_ref[...])
    ps = _dot(hb, ws_ref[...])
    su_ref[0] = _dot(hb, wu_ref[...])

    fgl = _dot(hb, wgh_ref[...]) + _dot(hb, wgl_ref[...]) + _dot(hl, wgh_ref[...])
    lane = lax.broadcasted_iota(jnp.int32, (tm, LANES), 1)
    lf = jnp.where(lane < N_HEADS, -_softplus(-(fgl + bf_ref[...])), 0.0)
    lt = ltri_ref[...]
    l_hi, l_mid, l_lo = _split3(lf)
    c = _dot(lt, l_hi) + _dot(lt, l_mid) + _dot(lt, l_lo) + carry_ref[...]
    carry_ref[...] = c[tm - 1:tm, :]
    c_ref[0] = c

    cparts = jnp.concatenate(_split3(c), axis=1)
    augq = _dot(cparts, selq_ref[...]) + cq_ref[...]
    augk = _dot(cparts, selk_ref[...]) + ck_ref[...]

    w = ATTN_WIDTH
    fq, fk, fv = pf[:, 0:w], pf[:, w:2 * w], pf[:, 2 * w:3 * w]
    bd = bd_ref[...]
    qn = fq * lax.rsqrt(_dot((fq * fq).astype(BF16), bd) + EPS) * gq_ref[...]
    kn = fk * lax.rsqrt(_dot((fk * fk).astype(BF16), bd) + EPS) * gk_ref[...]
    sq, sk, sv = ps[:, 0:w] * QK_SCALE, ps[:, w:2 * w], ps[:, 2 * w:3 * w]

    for h_idx in range(N_HEADS):
        p, hh = divmod(h_idx, 2)
        in_head = (lane < HEAD_DIM) if hh == 0 else (lane >= HEAD_DIM)
        one_lane = HEAD_DIM if hh == 0 else 0
        ts = slice(LANES * p, LANES * (p + 1))
        ah = slice(LANES * h_idx, LANES * (h_idx + 1))
        qa_ref[0, h_idx] = jnp.where(in_head, qn[:, ts], augq[:, ah]).astype(BF16)
        ka_ref[0, h_idx] = jnp.where(in_head, kn[:, ts], augk[:, ah]).astype(BF16)
        ones_col = jnp.where(lane == one_lane, 1.0, 0.0)
        va_ref[0, h_idx] = jnp.where(in_head, fv[:, ts], ones_col).astype(BF16)
        sq_ref[0, h_idx] = jnp.where(in_head, sq[:, ts], 0.0).astype(BF16)
        sk_ref[0, h_idx] = jnp.where(in_head, sk[:, ts], 0.0).astype(BF16)
        sv_ref[0, h_idx] = jnp.where(in_head, sv[:, ts], 0.0).astype(BF16)


def _aug_tables():
    selq = np.zeros((3 * LANES, N_HEADS * LANES), np.float32)
    selk = np.zeros((3 * LANES, N_HEADS * LANES), np.float32)
    cq = np.zeros((1, N_HEADS * LANES), np.float32)
    ck = np.zeros((1, N_HEADS * LANES), np.float32)
    shift = np.zeros((1, N_HEADS * LANES), np.float32)
    for h in range(N_HEADS):
        base = LANES * h + (HEAD_DIM if h % 2 == 0 else 0)
        for piece in range(3):
            selq[piece * LANES + h, base + piece] = 1.0
            selk[piece * LANES + h, base + 3 + piece] = -1.0
            cq[0, base + 3 + piece] = 1.0
            ck[0, base + piece] = 1.0
        cq[0, base + 6] = 1.0
        shift[0, base + 6] = 1.0
    return selq, selk, cq, ck, shift


def _inproj(x, gain, w_in, b_forget, q_gain, k_gain, bound):
    bsz, s_len, d = x.shape
    tm = TM_PROJ
    w = ATTN_WIDTH
    o_fg = 3 * w
    o_s = o_fg + N_HEADS
    o_u = o_s + 3 * w
    wf = w_in[:, :o_fg].astype(BF16)
    ws = w_in[:, o_s:o_u].astype(BF16)
    wu = w_in[:, o_u:].astype(BF16)
    wg = jnp.pad(w_in[:, o_fg:o_s], ((0, 0), (0, LANES - N_HEADS)))
    wgh, wgl = _split2(wg)
    bfp = jnp.pad(b_forget.astype(F32), (0, LANES - N_HEADS))[None, :]
    ltri = jnp.asarray(np.tril(np.ones((tm, tm), np.float32)), BF16)
    bd = jnp.asarray(np.kron(np.eye(N_HEADS, dtype=np.float32),
                             np.full((HEAD_DIM, HEAD_DIM), 1.0 / HEAD_DIM, np.float32)), BF16)
    gq = (jnp.tile(q_gain.astype(F32), N_HEADS) * QK_SCALE)[None, :]
    gk = jnp.tile(k_gain.astype(F32), N_HEADS)[None, :]
    selq, selk, cq, ck, shift = _aug_tables()
    selq, selk = jnp.asarray(selq, BF16), jnp.asarray(selk, BF16)
    cq = jnp.asarray(cq)
    ck = jnp.asarray(ck) - bound.astype(BF16).astype(F32) * jnp.asarray(shift)

    def full(a):
        return pl.BlockSpec(a.shape, lambda b, i: (0,) * a.ndim)

    consts = (gain.astype(F32)[None, :], wf, ws, wu, wgh, wgl, bfp, ltri, bd, gq, gk,
              selq, selk, cq, ck)
    head_shape = jax.ShapeDtypeStruct((bsz, N_HEADS, s_len, LANES), BF16)
    head_spec = pl.BlockSpec((1, N_HEADS, tm, LANES), lambda b, i: (b, 0, i, 0))
    return pl.pallas_call(
        _inproj_kernel,
        grid=(bsz, s_len // tm),
        in_specs=[pl.BlockSpec((1, tm, d), lambda b, i: (b, i, 0))] + [full(a) for a in consts],
        out_specs=[head_spec] * 6 + [
            pl.BlockSpec((1, tm, SSM_WIDTH), lambda b, i: (b, i, 0)),
            pl.BlockSpec((1, tm, LANES), lambda b, i: (b, i, 0))],
        out_shape=[head_shape] * 6 + [
            jax.ShapeDtypeStruct((bsz, s_len, SSM_WIDTH), F32),
            jax.ShapeDtypeStruct((bsz, s_len, LANES), F32)],
        scratch_shapes=[pltpu.VMEM((1, LANES), F32)],
        compiler_params=_cparams(("arbitrary", "arbitrary")),
        name="inproj",
    )(x, *consts)


def _pair_rms(y, gain):
    lane = lax.broadcasted_iota(jnp.int32, y.shape, 1)
    lo = lane < HEAD_DIM
    y2 = y * y
    ms0 = jnp.sum(jnp.where(lo, y2, 0.0), axis=-1, keepdims=True) * (1.0 / HEAD_DIM)
    ms1 = jnp.sum(jnp.where(lo, 0.0, y2), axis=-1, keepdims=True) * (1.0 / HEAD_DIM)
    return y * jnp.where(lo, lax.rsqrt(ms0 + EPS), lax.rsqrt(ms1 + EPS)) * gain


def _fox_kernel(kvs_ref, qa_ref, ka_ref, va_ref, gain_ref, o_ref, acc_ref, m_ref, *, nq, bounded):
    tq = qa_ref.shape[2]
    b, p, i = pl.program_id(0), pl.program_id(1), pl.program_id(2)
    row = lax.broadcasted_iota(jnp.int32, (tq, tq), 0)
    col = lax.broadcasted_iota(jnp.int32, (tq, tq), 1)

    def step(j, masked, hh):
        off = pl.multiple_of(j * tq, tq)
        k = ka_ref[0, hh, pl.ds(off, tq), :]
        v = va_ref[0, hh, pl.ds(off, tq), :]
        s = _dot_nt(qa_ref[0, hh], k)
        if masked:
            s = jnp.where(col <= row, s, NEG_BIG)
        if bounded:
            acc_ref[hh] += _dot(jnp.exp(s).astype(BF16), v)
            return
        m_prev = m_ref[hh]
        m_new = jnp.maximum(m_prev, jnp.max(s, axis=-1, keepdims=True))
        pexp = jnp.exp(s - m_new)
        acc_ref[hh] = jnp.exp(m_prev - m_new) * acc_ref[hh] + _dot(pexp.astype(BF16), v)
        m_ref[hh] = m_new

    def sweep(lo, hi, heads):
        def body(j, carry):
            for hh in heads:
                step(j, False, hh)
            return carry

        lax.fori_loop(lo, hi, body, 0)

    starts = [kvs_ref[(b * N_HEADS + 2 * p + hh) * nq + i] for hh in range(2)]
    for hh in range(2):
        acc_ref[hh] = jnp.zeros((tq, LANES), F32)
        if not bounded:
            m_ref[hh] = jnp.full((tq, 1), NEG_BIG, F32)
    both = jnp.maximum(starts[0], starts[1])
    sweep(starts[0], both, (0,))
    sweep(starts[1], both, (1,))
    sweep(both, i, (0, 1))
    step(i, True, 0)
    step(i, True, 1)

    lane = lax.broadcasted_iota(jnp.int32, (tq, LANES), 1)
    a0, a1 = acc_ref[0], acc_ref[1]
    y = jnp.where(lane < HEAD_DIM, a0 / a0[:, HEAD_DIM:HEAD_DIM + 1], a1 / a1[:, 0:1])
    o_ref[0] = _pair_rms(y, gain_ref[0]).astype(BF16)


def _fox_call(kv_start, qa, ka, va, gain3, *, bounded):
    bsz, _, s_len, _ = qa.shape
    tq = TQ
    nq = s_len // tq
    kv_spec = pl.BlockSpec((1, 2, s_len, LANES), lambda b, p, i, kvs: (b, p, 0, 0))
    return pl.pallas_call(
        functools.partial(_fox_kernel, nq=nq, bounded=bounded),
        grid_spec=pltpu.PrefetchScalarGridSpec(
            num_scalar_prefetch=1,
            grid=(bsz, N_PAIRS, nq),
            in_specs=[pl.BlockSpec((1, 2, tq, LANES), lambda b, p, i, kvs: (b, p, i, 0)),
                      kv_spec, kv_spec,
                      pl.BlockSpec((1, 1, LANES), lambda b, p, i, kvs: (p, 0, 0))],
            out_specs=pl.BlockSpec((1, tq, LANES), lambda b, p, i, kvs: (b, i, p)),
            scratch_shapes=[pltpu.VMEM((2, tq, LANES), F32), pltpu.VMEM((2, tq, 1), F32)]),
        out_shape=jax.ShapeDtypeStruct((bsz, s_len, ATTN_WIDTH), BF16),
        compiler_params=_cparams(("arbitrary", "arbitrary", "arbitrary")),
        name="fox_attention_bounded" if bounded else "fox_attention",
    )(kv_start, qa, ka, va, gain3)


def _fox(qa, ka, va, kv_start, gain3, bound):
    return lax.cond(bound <= FOX_MAX_BOUND,
                    functools.partial(_fox_call, bounded=True),
                    functools.partial(_fox_call, bounded=False),
                    kv_start, qa, ka, va, gain3)


def _fox_bound(q_gain, k_gain):
    return 1.05 * HEAD_DIM * QK_SCALE * jnp.max(jnp.abs(q_gain)) * jnp.max(jnp.abs(k_gain))


def _fox_kv_start(c, bound, s_len):
    tq = TQ
    nq = s_len // tq
    ch = c[:, :, :N_HEADS]
    c_first = ch[:, 0::tq, :]
    c_end = ch[:, tq - 1::tq, :]
    skip = (c_first[:, :, None, :] - c_end[:, None, :, :]) < (LOG_ZERO - 2.0 * bound)
    start = jnp.sum(skip.astype(jnp.int32), axis=2)
    start = jnp.minimum(start, jnp.arange(nq, dtype=jnp.int32)[None, :, None])
    return start.transpose(0, 2, 1).reshape(-1)


def _sb_kernel(q_ref, k_ref, v_ref, u_ref, gain_ref, o_ref, acc_ref, r_ref):
    tq = q_ref.shape[2]
    i = pl.program_id(2)
    row = lax.broadcasted_iota(jnp.int32, (tq, tq), 0)
    col = lax.broadcasted_iota(jnp.int32, (tq, tq), 1)

    for hh in range(2):
        q = q_ref[0, hh]
        acc_ref[hh] = jnp.zeros((tq, LANES), F32)
        r_ref[hh] = jnp.zeros((tq, 1), F32)

        def step(j, masked, hh=hh, q=q):
            off = pl.multiple_of(j * tq, tq)
            k = k_ref[0, hh, pl.ds(off, tq), :]
            v = v_ref[0, hh, pl.ds(off, tq), :]
            z = _dot_nt(q, k)
            sp = jnp.maximum(z, 0.0) + jnp.log(1.0 + jnp.exp(-jnp.abs(z)))
            log_not = jnp.where(col < row, -sp, 0.0) if masked else -sp
            u = u_ref[...]
            pieces = []
            tail = jnp.zeros((tq, 1), F32)
            for c0 in range(tq - SB_SUB, -1, -SB_SUB):
                part = log_not[:, c0:c0 + SB_SUB]
                n_hi, n_lo = _split2(part)
                later_part = _dot(n_hi, u) + _dot(n_lo, u) + tail
                tail = later_part[:, 0:1] + part[:, 0:1]
                pieces.insert(0, later_part)
            later_in = jnp.concatenate(pieces, axis=1)
            r_prev = r_ref[hh]
            a = jnp.exp(z - sp + later_in + r_prev)
            if masked:
                a = jnp.where(col < row, a, 0.0)
            acc_ref[hh] += _dot(a.astype(BF16), v)
            r_new = r_prev + tail
            r_ref[hh] = r_new
            return jnp.max(r_new)

        def cond(carry):
            j, r_max = carry
            return jnp.logical_and(j >= 0, r_max >= LOG_ZERO)

        def body(carry):
            j, _ = carry
            return j - 1, step(j, False)

        lax.while_loop(cond, body, (i - 1, step(i, True)))

    lane = lax.broadcasted_iota(jnp.int32, (tq, LANES), 1)
    y = jnp.where(lane < HEAD_DIM, acc_ref[0], acc_ref[1])
    o_ref[0] = _pair_rms(y, gain_ref[0]).astype(BF16)


def _sb(sq, sk, sv, gain3):
    bsz, _, s_len, _ = sq.shape
    tq = TQ_SB
    upper =jnp.asarray(np.tril(np.ones((SB_SUB, SB_SUB), np.float32), -1), BF16)
    kv_spec = pl.BlockSpec((1, 2, s_len, LANES), lambda b, p, i: (b, p, 0, 0))
    return pl.pallas_call(
        _sb_kernel,
        grid=(bsz, N_PAIRS, s_len // tq),
        in_specs=[pl.BlockSpec((1, 2, tq, LANES), lambda b, p, i: (b, p, i, 0)),
                  kv_spec, kv_spec,
                  pl.BlockSpec((SB_SUB, SB_SUB), lambda b, p, i: (0, 0)),
                  pl.BlockSpec((1, 1, LANES), lambda b, p, i: (p, 0, 0))],
        out_specs=pl.BlockSpec((1, tq, LANES), lambda b, p, i: (b, i, p)),
        out_shape=jax.ShapeDtypeStruct((bsz, s_len, ATTN_WIDTH), BF16),
        scratch_shapes=[pltpu.VMEM((2, tq, LANES), F32), pltpu.VMEM((2, tq, 1), F32)],
        compiler_params=_cparams(("arbitrary", "arbitrary", "arbitrary")),
        name="stickbreak_attention",
    )(sq, sk, sv, upper, gain3)


def _ssm_kernel(u_ref, bre_ref, bim_ref, ccat_ref, pw_ref, d_ref, wglu_ref, bglu_ref,
                bd_ref, gain_ref, o_ref, xr_ref, xi_ref, hc_ref):
    chunk = u_ref.shape[1]

    @pl.when(pl.program_id(1) == 0)
    def _():
        hc_ref[...] = jnp.zeros_like(hc_ref)

    u = u_ref[0]
    ub = u.astype(BF16)
    xr_ref[...] = _dot(ub, bre_ref[...])
    xi_ref[...] = _dot(ub, bim_ref[...])

    def group(g, carry):
        hr, hi = carry
        rows = pl.ds(pl.multiple_of(g * 8, 8), 8)
        xr, xi = xr_ref[rows, :], xi_ref[rows, :]
        for k, d in enumerate((1, 2, 4)):
            mr, mi = pw_ref[2 + 2 * k], pw_ref[3 + 2 * k]
            sr, si = pltpu.roll(xr, d, 0), pltpu.roll(xi, d, 0)
            xr, xi = xr + mr * sr - mi * si, xi + mr * si + mi * sr
        pr, pi = pw_ref[0], pw_ref[1]
        xr, xi = xr + pr * hr - pi * hi, xi + pr * hi + pi * hr
        xr_ref[rows, :] = xr
        xi_ref[rows, :] = xi
        return xr[7:8, :], xi[7:8, :]

    hr, hi = lax.fori_loop(0, chunk // 8, group, (hc_ref[0:1, :], hc_ref[1:2, :]))
    hc_ref[0:1, :] = hr
    hc_ref[1:2, :] = hi

    hcat = jnp.concatenate([xr_ref[...], xi_ref[...]], axis=1).astype(BF16)
    y = _dot(hcat, ccat_ref[...]) + d_ref[...] * u
    g = 0.5 * y * (1.0 + jnp.tanh(math.sqrt(2.0 / math.pi) * (y + 0.044715 * (y * y * y))))
    g = g * jax.nn.sigmoid(_dot(g.astype(BF16), wglu_ref[...]) + bglu_ref[...])
    ms = _dot((g * g).astype(BF16), bd_ref[...])
    o_ref[0] = (g * lax.rsqrt(ms + EPS) * gain_ref[...]).astype(BF16)


def _ssm_tables(a_re, a_im, log_dt, b_re, b_im, c_re, c_im):
    a_re = a_re.astype(F32)
    a_im = a_im.astype(F32)
    dt = jnp.exp(log_dt.astype(F32))[:, None]
    mag = jnp.exp(dt * a_re)
    ang = dt * a_im
    abar_re = mag * jnp.cos(ang)
    abar_im = mag * jnp.sin(ang)
    num_re = abar_re - 1.0
    num_im = abar_im
    den = a_re * a_re + a_im * a_im
    coef_re = (num_re * a_re + num_im * a_im) / den
    coef_im = (num_im * a_re - num_re * a_im) / den
    b_re = b_re.astype(F32)
    b_im = b_im.astype(F32)
    bbar_re = coef_re[..., None] * b_re - coef_im[..., None] * b_im
    bbar_im = coef_re[..., None] * b_im + coef_im[..., None] * b_re
    eye = jnp.eye(SSM_GROUPS, dtype=F32)

    def b_blockdiag(bb):
        return jnp.einsum('gnc,gh->gchn', bb, eye).reshape(SSM_WIDTH, SSM_LANES).astype(BF16)

    def c_blockdiag(cc):
        return jnp.einsum('gcn,gh->gnhc', cc.astype(F32), eye).reshape(SSM_LANES, SSM_WIDTH)

    ccat = jnp.concatenate([c_blockdiag(c_re), -c_blockdiag(c_im)], axis=0).astype(BF16)

    ar = abar_re.reshape(1, SSM_LANES)
    ai = abar_im.reshape(1, SSM_LANES)
    pows = [(ar, ai)]
    for _ in range(7):
        pr, pi = pows[-1]
        pows.append((pr * ar - pi * ai, pr * ai + pi * ar))
    p_re = jnp.concatenate([p[0] for p in pows], axis=0)
    p_im = jnp.concatenate([p[1] for p in pows], axis=0)
    rows = jnp.arange(8)[:, None]
    tabs = [p_re, p_im]
    for d in (1, 2, 4):
        tabs.append(jnp.where(rows >= d, pows[d - 1][0], 0.0))
        tabs.append(jnp.where(rows >= d, pows[d - 1][1], 0.0))
    return b_blockdiag(bbar_re), b_blockdiag(bbar_im), ccat, jnp.stack(tabs, axis=0)


def _ssm(su, tables, d_skip, w_glu, b_glu, gain):
    bsz, s_len, _ = su.shape
    chunk = SSM_CHUNK
    bre, bim, ccat, pw = tables
    bd = jnp.asarray(np.kron(np.eye(SSM_GROUPS, dtype=np.float32),
                             np.full((SSM_GROUP, SSM_GROUP), 1.0 / SSM_GROUP, np.float32)), BF16)
    consts = (bre, bim, ccat, pw, d_skip.astype(F32)[None, :], w_glu.astype(BF16),
              b_glu.astype(F32)[None, :], bd, gain.astype(F32)[None, :])

    def full(a):
        return pl.BlockSpec(a.shape, lambda b, i: (0,) * a.ndim)

    return pl.pallas_call(
        _ssm_kernel,
        grid=(bsz, s_len // chunk),
        in_specs=[pl.BlockSpec((1, chunk, SSM_WIDTH), lambda b, i: (b, i, 0))]
        + [full(a) for a in consts],
        out_specs=pl.BlockSpec((1, chunk, SSM_WIDTH), lambda b, i: (b, i, 0)),
        out_shape=jax.ShapeDtypeStruct((bsz, s_len, SSM_WIDTH), BF16),
        scratch_shapes=[pltpu.VMEM((chunk, SSM_LANES), F32), pltpu.VMEM((chunk, SSM_LANES), F32),
                        pltpu.VMEM((2, SSM_LANES), F32)],
        compiler_params=_cparams(("arbitrary", "arbitrary")),
        name="s5_mixer",
    )(su, *consts)


def _outproj_kernel(*refs, routed):
    if routed:
        yf_ref, ys_ref, ym_ref, x_ref, wo_ref, g_ref, wrh_ref, wrl_ref, xo_ref, h_ref, rt_ref = refs
    else:
        yf_ref, ys_ref, ym_ref, x_ref, wo_ref, g_ref, xo_ref, h_ref = refs
    mix = jnp.concatenate([yf_ref[...], ys_ref[...], ym_ref[...]], axis=1)
    x = x_ref[...] + _dot(mix, wo_ref[...])
    xo_ref[...] = x
    h = x * lax.rsqrt(jnp.mean(x * x, axis=-1, keepdims=True) + EPS) * g_ref[...]
    if not routed:
        h_ref[...] = h.astype(BF16)
        return
    _store_token_tiles(h_ref, h)
    hb, hl = _split2(h)
    logits = _dot(hb, wrh_ref[...]) + _dot(hb, wrl_ref[...]) + _dot(hl, wrh_ref[...])
    lane = lax.broadcasted_iota(jnp.int32, logits.shape, 1).astype(F32)
    logits = jnp.where(lane < N_EXPERTS, logits, NEG_BIG)
    m1 = jnp.max(logits, axis=-1, keepdims=True)
    i1 = jnp.min(jnp.where(logits == m1, lane, float(LANES)), axis=-1, keepdims=True)
    rest = jnp.where(lane == i1, NEG_BIG, logits)
    m2 = jnp.max(rest, axis=-1, keepdims=True)
    i2 = jnp.min(jnp.where(rest == m2, lane, float(LANES)), axis=-1, keepdims=True)
    e2 = jnp.exp(m2 - m1)
    w1 = 1.0 / (1.0 + e2)
    w2 = e2 / (1.0 + e2)
    rt_ref[...] = jnp.where(lane == 0, i1, jnp.where(lane == 1, i2, jnp.where(
        lane == 2, w1, jnp.where(lane == 3, w2, 0.0))))


def _outproj(yf, ys, ym, x, w_out, ffn_gain, w_router=None):
    t, d = x.shape
    tm = TM_PROJ
    routed = w_router is not None
    consts = [w_out.astype(BF16), ffn_gain.astype(F32)[None, :]]
    if routed:
        consts += list(_split2(jnp.pad(w_router.astype(F32), ((0, 0), (0, LANES - N_EXPERTS)))))

    def rows(width):
        return pl.BlockSpec((tm, width), lambda i: (i, 0))

    def full(a):
        return pl.BlockSpec(a.shape, lambda i: (0,) * a.ndim)

    if routed:
        out_specs = [rows(d), pl.BlockSpec((tm * SUBLANES, LANES), lambda i: (i, 0)), rows(LANES)]
        out_shape = [jax.ShapeDtypeStruct((t, d), F32),
                     jax.ShapeDtypeStruct((t * SUBLANES, LANES), F32),
                     jax.ShapeDtypeStruct((t, LANES), F32)]
    else:
        out_specs = [rows(d), rows(d)]
        out_shape = [jax.ShapeDtypeStruct((t, d), F32), jax.ShapeDtypeStruct((t, d), BF16)]
    return pl.pallas_call(
        functools.partial(_outproj_kernel, routed=routed),
        grid=(t // tm,),
        in_specs=[rows(ATTN_WIDTH), rows(ATTN_WIDTH), rows(SSM_WIDTH), rows(d)]
        + [full(a) for a in consts],
        out_specs=out_specs,
        out_shape=out_shape,
        compiler_params=_cparams(("arbitrary",)),
        name="outproj_routed" if routed else "outproj",
    )(yf, ys, ym, x, *consts)


def _swiglu_chunk(h, wg_ref, wu_ref, wd_ref):
    g = _dot(h, wg_ref[...].astype(BF16))
    u = _dot(h, wu_ref[...].astype(BF16))
    return _dot((g * jax.nn.sigmoid(g) * u).astype(BF16), wd_ref[...].astype(BF16))


def _ffn_kernel(h_ref, x_ref, wg_ref, wu_ref, wd_ref, o_ref):
    @pl.when(pl.program_id(1) == 0)
    def _():
        o_ref[...] = x_ref[...]

    o_ref[...] += _swiglu_chunk(h_ref[...], wg_ref, wu_ref, wd_ref)


def _ffn(h, x, w_gate, w_up, w_down):
    t, d = x.shape
    d_ff = w_gate.shape[1]
    tm, tf = TM_FFN, TF_FFN
    return pl.pallas_call(
        _ffn_kernel,
        grid=(t // tm, d_ff // tf),
        in_specs=[pl.BlockSpec((tm, d), lambda i, f: (i, 0)),
                  pl.BlockSpec((tm, d), lambda i, f: (i, 0)),
                  pl.BlockSpec((d, tf), lambda i, f: (0, f)),
                  pl.BlockSpec((d, tf), lambda i, f: (0, f)),
                  pl.BlockSpec((tf, d), lambda i, f: (f, 0))],
        out_specs=pl.BlockSpec((tm, d), lambda i, f: (i, 0)),
        out_shape=jax.ShapeDtypeStruct((t, d), F32),
        compiler_params=_cparams(("arbitrary", "arbitrary")),
        name="dense_swiglu",
    )(h, x, w_gate, w_up, w_down)


def _tile_copy(src_hbm, dst_ref, src_tok, dst_tok, sem):
    return pltpu.make_async_copy(
        src_hbm.at[pl.ds(pl.multiple_of(src_tok * SUBLANES, SUBLANES), SUBLANES), :],
        dst_ref.at[pl.ds(pl.multiple_of(dst_tok * SUBLANES, SUBLANES), SUBLANES), :], sem)


def _start_gather(idx_ref, src_hbm, dst_ref, sem, n_tok):
    def body(r, carry):
        _tile_copy(src_hbm, dst_ref, idx_ref[0, 0, r], r, sem).start()
        return carry

    lax.fori_loop(0, n_tok, body, 0, unroll=8)


def _wait_gather(idx_ref, src_hbm, dst_ref, sem, n_tok):
    def body(r, carry):
        _tile_copy(src_hbm, dst_ref, idx_ref[0, 0, r], r, sem).wait()
        return carry

    lax.fori_loop(0, n_tok, body, 0, unroll=8)


def _grouped_ffn_kernel(te_ref, tv_ref, idx_ref, idx_next_ref, h_hbm, wg_ref, wu_ref, wd_ref,
                        o_ref, xs_ref, hb_ref, acc_ref, sem):
    i, f = pl.program_id(0), pl.program_id(1)
    n_tiles, nf = pl.num_programs(0), pl.num_programs(1)
    tm = hb_ref.shape[0]
    slot = i % 2
    live = tv_ref[i] > 0

    @pl.when(jnp.logical_and(i == 0, f == 0))
    def _():
        _start_gather(idx_ref, h_hbm, xs_ref.at[0], sem.at[0], tm)

    @pl.when(jnp.logical_and(f == 0, live))
    def _():
        _wait_gather(idx_ref, h_hbm, xs_ref.at[slot], sem.at[slot], tm)
        for s, part in enumerate(_load_token_tiles(xs_ref.at[slot], tm)):
            hb_ref[:, LANES * s:LANES * (s + 1)] = part.astype(BF16)

    @pl.when(jnp.logical_and(f == 0, tv_ref[jnp.minimum(i + 1, n_tiles - 1)] > 0))
    def _():
        @pl.when(i + 1 < n_tiles)
        def _():
            _start_gather(idx_next_ref, h_hbm, xs_ref.at[1 - slot], sem.at[1 - slot], tm)

    @pl.when(f == 0)
    def _():
        acc_ref[...] = jnp.zeros_like(acc_ref)

    @pl.when(live)
    def _():
        acc_ref[...] += _swiglu_chunk(hb_ref[...], wg_ref.at[0], wu_ref.at[0], wd_ref.at[0])

    @pl.when(f == nf - 1)
    def _():
        _store_token_tiles(o_ref, acc_ref[...])


def _moe_grouped_ffn(h_tiles, src_tok, tile_expert, tile_valid, w_gate, w_up, w_down):
    n_tiles = tile_expert.shape[0]
    d, d_ff = w_gate.shape[1], w_gate.shape[2]
    tm, tf = TM_FFN, TF_FFN
    nf = d_ff // tf
    idx = src_tok.reshape(n_tiles, 1, tm)

    def f_idx(i, f, te, tv):
        return jnp.where(tv[i] > 0, f, nf - 1)

    return pl.pallas_call(
        _grouped_ffn_kernel,
        grid_spec=pltpu.PrefetchScalarGridSpec(
            num_scalar_prefetch=2,
            grid=(n_tiles, nf),
            in_specs=[pl.BlockSpec((1, 1, tm), lambda i, f, te, tv: (i, 0, 0), memory_space=pltpu.SMEM),
                      pl.BlockSpec((1, 1, tm), lambda i, f, te, tv: (jnp.minimum(i + 1, n_tiles - 1), 0, 0),
                                   memory_space=pltpu.SMEM),
                      pl.BlockSpec(memory_space=pl.ANY),
                      pl.BlockSpec((1, d, tf), lambda i, f, te, tv: (te[i], 0, f_idx(i, f, te, tv))),
                      pl.BlockSpec((1, d, tf), lambda i, f, te, tv: (te[i], 0, f_idx(i, f, te, tv))),
                      pl.BlockSpec((1, tf, d), lambda i, f, te, tv: (te[i], f_idx(i, f, te, tv), 0))],
            out_specs=pl.BlockSpec((tm * SUBLANES, LANES), lambda i, f, te, tv: (i, 0)),
            scratch_shapes=[pltpu.VMEM((2, tm * SUBLANES, LANES), F32), pltpu.VMEM((tm, d), BF16),
                            pltpu.VMEM((tm, d), F32), pltpu.SemaphoreType.DMA((2,))]),
        out_shape=jax.ShapeDtypeStruct((n_tiles * tm * SUBLANES, LANES), F32),
        compiler_params=_cparams(("arbitrary", "arbitrary")),
        name="moe_grouped_swiglu",
    )(tile_expert, tile_valid, idx, idx, h_tiles, w_gate, w_up, w_down)


def _combine_kernel(pos_ref, pos_next_ref, ys_hbm, x_ref, rt_ref, o_ref, buf_ref, sem):
    i, n_steps = pl.program_id(0), pl.num_programs(0)
    tc = x_ref.shape[0]
    slot = i % 2

    @pl.when(i == 0)
    def _():
        _start_gather(pos_ref, ys_hbm, buf_ref.at[0], sem.at[0], 2 * tc)

    @pl.when(i + 1 < n_steps)
    def _():
        _start_gather(pos_next_ref, ys_hbm, buf_ref.at[1 - slot], sem.at[1 - slot], 2 * tc)

    _wait_gather(pos_ref, ys_hbm, buf_ref.at[slot], sem.at[slot], 2 * tc)
    rt = rt_ref[...]
    w0, w1 = rt[:, 2:3], rt[:, 3:4]
    y0 = _load_token_tiles(buf_ref.at[slot], tc)
    y1 = _load_token_tiles(buf_ref.at[slot], tc, base=tc * SUBLANES)
    for s in range(SUBLANES):
        cols = slice(LANES * s, LANES * (s + 1))
        o_ref[:, cols] = x_ref[:, cols] + w0 * y0[s] + w1 * y1[s]


def _moe_combine(ys, pos, x, route):
    t, d = x.shape
    tc = TC_COMBINE
    n_steps = t // tc
    return pl.pallas_call(
        _combine_kernel,
        grid=(n_steps,),
        in_specs=[pl.BlockSpec((1, 1, 2 * tc), lambda i: (i, 0, 0), memory_space=pltpu.SMEM),
                  pl.BlockSpec((1, 1, 2 * tc), lambda i: (jnp.minimum(i + 1, n_steps - 1), 0, 0),
                               memory_space=pltpu.SMEM),
                  pl.BlockSpec(memory_space=pl.ANY),
                  pl.BlockSpec((tc, d), lambda i: (i, 0)),
                  pl.BlockSpec((tc, LANES), lambda i: (i, 0))],
        out_specs=pl.BlockSpec((tc, d), lambda i: (i, 0)),
        out_shape=jax.ShapeDtypeStruct((t, d), F32),
        scratch_shapes=[pltpu.VMEM((2, 2 * tc * SUBLANES, LANES), F32), pltpu.SemaphoreType.DMA((2,))],
        compiler_params=_cparams(("arbitrary",)),
        name="moe_combine",
    )(pos, pos, ys, x, route)


def _moe(h, x, route, w_gate, w_up, w_down):
    t = x.shape[0]
    tm, tc = TM_FFN, TC_COMBINE
    n_assign = TOP_K * t
    n_rows = n_assign + N_EXPERTS * tm
    expert = route[:, 0:TOP_K].astype(jnp.int32).reshape(-1)
    onehot = (expert[:, None] == jnp.arange(N_EXPERTS, dtype=jnp.int32)[None, :]).astype(jnp.int32)
    rank = jnp.sum((jnp.cumsum(onehot, axis=0) - onehot) * onehot, axis=1)
    counts = jnp.sum(onehot, axis=0)
    padded = ((counts + tm - 1) // tm) * tm
    group_end = jnp.cumsum(padded)
    dest = (group_end - padded)[expert] + rank
    src_tok = jnp.zeros((n_rows,), jnp.int32).at[dest].set(jnp.arange(n_assign, dtype=jnp.int32) // TOP_K)
    tile_first = jnp.arange(n_rows // tm, dtype=jnp.int32) * tm
    tile_valid = (tile_first < group_end[-1]).astype(jnp.int32)
    tile_expert = jnp.minimum(jnp.searchsorted(group_end, tile_first, side='right'),
                              N_EXPERTS - 1).astype(jnp.int32)
    last_expert = jnp.max(jnp.where(tile_valid > 0, tile_expert, 0))
    tile_expert = jnp.where(tile_valid > 0, tile_expert, last_expert)
    pos = dest.reshape(t // tc, tc, TOP_K).transpose(0, 2, 1).reshape(t // tc, 1, TOP_K * tc)

    ys = _moe_grouped_ffn(h, src_tok, tile_expert, tile_valid, w_gate, w_up, w_down)
    return _moe_combine(ys, pos, x, route)


@jax.jit
def _forward(x, attn_norm, ffn_norm, w_in, b_forget, fox_q_norm, fox_k_norm,
             ssm_a_re, ssm_a_im, ssm_log_dt, ssm_b_re, ssm_b_im, ssm_c_re,
             ssm_c_im, ssm_d, ssm_w_glu, ssm_b_glu, out_norm, w_out,
             dense_w_gate, dense_w_up, dense_w_down, moe_w_router,
             moe_w_gate, moe_w_up, moe_w_down):
    bsz, s_len, d = x.shape
    depth = w_in.shape[0]
    x = x.astype(F32)
    for i in range(depth):
        bound = _fox_bound(fox_q_norm[i], fox_k_norm[i])
        qa, ka, va, sq, sk, sv, su, c = _inproj(x, attn_norm[i], w_in[i], b_forget[i],
                                                fox_q_norm[i], fox_k_norm[i], bound)
        gain = out_norm[i].astype(F32)
        kv_start = _fox_kv_start(c, bound, s_len)
        y_fox = _fox(qa, ka, va, kv_start, gain[:ATTN_WIDTH].reshape(N_PAIRS, 1, LANES), bound)
        y_sb = _sb(sq, sk, sv, gain[ATTN_WIDTH:2 * ATTN_WIDTH].reshape(N_PAIRS, 1, LANES))
        tables = _ssm_tables(ssm_a_re[i], ssm_a_im[i], ssm_log_dt[i], ssm_b_re[i], ssm_b_im[i],
                             ssm_c_re[i], ssm_c_im[i])
        y_ssm = _ssm(su, tables, ssm_d[i], ssm_w_glu[i], ssm_b_glu[i], gain[2 * ATTN_WIDTH:])
        t = bsz * s_len
        flat = (y_fox.reshape(t, -1), y_sb.reshape(t, -1), y_ssm.reshape(t, -1), x.reshape(t, d))
        j = i // 2
        if i % 2 == 0:
            xo, h = _outproj(*flat, w_out[i], ffn_norm[i])
            x = _ffn(h, xo, dense_w_gate[j], dense_w_up[j], dense_w_down[j])
        else:
            xo, h, route = _outproj(*flat, w_out[i], ffn_norm[i], moe_w_router[j])
            x = _moe(h, xo, route, moe_w_gate[j], moe_w_up[j], moe_w_down[j])
        x = x.reshape(bsz, s_len, d)
    return x


def kernel(x, attn_norm, ffn_norm, w_in, b_forget, fox_q_norm, fox_k_norm, ssm_a_re, ssm_a_im,
           ssm_log_dt, ssm_b_re, ssm_b_im, ssm_c_re, ssm_c_im, ssm_d, ssm_w_glu, ssm_b_glu,
           out_norm, w_out, dense_w_gate, dense_w_up, dense_w_down, moe_w_router, moe_w_gate,
           moe_w_up, moe_w_down):
    return _forward(x, attn_norm, ffn_norm, w_in, b_forget, fox_q_norm, fox_k_norm, ssm_a_re,
                    ssm_a_im, ssm_log_dt, ssm_b_re, ssm_b_im, ssm_c_re, ssm_c_im, ssm_d,
                    ssm_w_glu, ssm_b_glu, out_norm, w_out, dense_w_gate, dense_w_up,
                    dense_w_down, moe_w_router, moe_w_gate, moe_w_up, moe_w_down)
```

```python
import functools
import math

import jax
import jax.numpy as jnp
import numpy as np
from jax import lax
from jax.experimental import pallas as pl
from jax.experimental.pallas import tpu as pltpu

F32 = jnp.float32
BF16 = jnp.bfloat16

EPS = 1e-6
LANES = 128
SUBLANES = 8
HEAD_DIM = 64
N_HEADS = 6
N_PAIRS = N_HEADS // 2
ATTN_WIDTH = N_HEADS * HEAD_DIM
SSM_GROUP = 16
SSM_GROUPS = 16
SSM_STATE = 64
SSM_WIDTH = SSM_GROUP * SSM_GROUPS
SSM_LANES = SSM_GROUPS * SSM_STATE
N_EXPERTS = 8
TOP_K = 2
QK_SCALE = HEAD_DIM ** -0.5
NEG_BIG = -1e30
LOG_ZERO = -105.0
FOX_MAX_BOUND = 40.0
VMEM_LIMIT = 56 * 1024 * 1024

TM_PROJ = 512
TQ = 512
SB_SUB = 256
TQ_SB = 2 * SB_SUB
SSM_CHUNK = 512
TM_FFN = 1024
TF_FFN = 512
TC_COMBINE = 256


def _dot(a, b):
    return jnp.dot(a, b, preferred_element_type=F32)


def _dot_nt(a, b):
    return lax.dot_general(a, b, (((1,), (1,)), ((), ())), preferred_element_type=F32)


def _split2(x):
    hi = x.astype(BF16)
    lo = (x - hi.astype(F32)).astype(BF16)
    return hi, lo


def _split3(x):
    hi = x.astype(BF16)
    r = x - hi.astype(F32)
    mid = r.astype(BF16)
    lo = (r - mid.astype(F32)).astype(BF16)
    return hi, mid, lo


def _softplus(z):
    return jnp.maximum(z, 0.0) + jnp.log1p(jnp.exp(-jnp.abs(z)))


def _store_token_tiles(ref, x, base=0):
    n = x.shape[0]
    for s in range(SUBLANES):
        ref[pl.ds(base + s, n, stride=SUBLANES), :] = x[:, LANES * s:LANES * (s + 1)]


def _load_token_tiles(ref, n, base=0):
    return [ref[pl.ds(base + s, n, stride=SUBLANES), :] for s in range(SUBLANES)]


def _cparams(sem):
    return pltpu.CompilerParams(dimension_semantics=sem, vmem_limit_bytes=VMEM_LIMIT)


def _inproj_kernel(x_ref, g_ref, wf_ref, ws_ref, wgh_ref, bf_ref,
                   ltri_ref, bd_ref, gq_ref, gk_ref, selq_ref, selk_ref, cq_ref, ck_ref,
                   qa_ref, ka_ref, va_ref, sq_ref, sk_ref, sv_ref, su_ref, c_ref,
                   carry_ref):
    tm = x_ref.shape[1]

    @pl.when(pl.program_id(1) == 0)
    def _():
        carry_ref[...] = jnp.zeros_like(carry_ref)

    x = x_ref[0]
    h = x * lax.rsqrt(jnp.mean(x * x, axis=-1, keepdims=True) + EPS) * g_ref[...]
    hb, hl = _split2(h)
    w = ATTN_WIDTH
    pf = _dot(hb, wf_ref[...])
    ps = _dot(hb, ws_ref[...])
    su_ref[0] = ps[:, 3 * w:3 * w + SSM_WIDTH]

    fgl = (pf[:, 3 * w:3 * w + LANES] + ps[:, 3 * w + SSM_WIDTH:3 * w + SSM_WIDTH + LANES]
           + _dot(hl, wgh_ref[...]))
    lane = lax.broadcasted_iota(jnp.int32, (tm, LANES), 1)
    lf = jnp.where(lane < N_HEADS, -_softplus(-(fgl + bf_ref[...])), 0.0)
    lt = ltri_ref[...]
    l_hi, l_mid, l_lo = _split3(lf)
    c = _dot(lt, l_hi) + _dot(lt, l_mid) + _dot(lt, l_lo) + carry_ref[...]
    carry_ref[...] = c[tm - 1:tm, :]
    c_ref[0] = c

    cparts = jnp.concatenate(_split3(c), axis=1)
    augq = _dot(cparts, selq_ref[...]) + cq_ref[...]
    augk = _dot(cparts, selk_ref[...]) + ck_ref[...]

    fq, fk, fv = pf[:, 0:w], pf[:, w:2 * w], pf[:, 2 * w:3 * w]
    bd = bd_ref[...]
    qn = fq * lax.rsqrt(_dot((fq * fq).astype(BF16), bd) + EPS) * gq_ref[...]
    kn = fk * lax.rsqrt(_dot((fk * fk).astype(BF16), bd) + EPS) * gk_ref[...]
    sq, sk, sv = ps[:, 0:w] * QK_SCALE, ps[:, w:2 * w], ps[:, 2 * w:3 * w]

    for h_idx in range(N_HEADS):
        p, hh = divmod(h_idx, 2)
        in_head = (lane < HEAD_DIM) if hh == 0 else (lane >= HEAD_DIM)
        one_lane = HEAD_DIM if hh == 0 else 0
        ts = slice(LANES * p, LANES * (p + 1))
        ah = slice(LANES * h_idx, LANES * (h_idx + 1))
        qa_ref[0, h_idx] = jnp.where(in_head, qn[:, ts], augq[:, ah]).astype(BF16)
        ka_ref[0, h_idx] = jnp.where(in_head, kn[:, ts], augk[:, ah]).astype(BF16)
        ones_col = jnp.where(lane == one_lane, 1.0, 0.0)
        va_ref[0, h_idx] = jnp.where(in_head, fv[:, ts], ones_col).astype(BF16)
        sq_ref[0, h_idx] = jnp.where(in_head, sq[:, ts], 0.0).astype(BF16)
        sk_ref[0, h_idx] = jnp.where(in_head, sk[:, ts], 0.0).astype(BF16)
        sv_ref[0, h_idx] = jnp.where(in_head, sv[:, ts], 0.0).astype(BF16)


def _aug_tables():
    selq = np.zeros((3 * LANES, N_HEADS * LANES), np.float32)
    selk = np.zeros((3 * LANES, N_HEADS * LANES), np.float32)
    cq = np.zeros((1, N_HEADS * LANES), np.float32)
    ck = np.zeros((1, N_HEADS * LANES), np.float32)
    shift = np.zeros((1, N_HEADS * LANES), np.float32)
    for h in range(N_HEADS):
        base = LANES * h + (HEAD_DIM if h % 2 == 0 else 0)
        for piece in range(3):
            selq[piece * LANES + h, base + piece] = 1.0
            selk[piece * LANES + h, base + 3 + piece] = -1.0
            cq[0, base + 3 + piece] = 1.0
            ck[0, base + piece] = 1.0
        cq[0, base + 6] = 1.0
        shift[0, base + 6] = 1.0
    return selq, selk, cq, ck, shift


def _inproj(x, gain, w_in, b_forget, q_gain, k_gain, bound):
    bsz, s_len, d = x.shape
    tm = TM_PROJ
    w = ATTN_WIDTH
    o_fg = 3 * w
    o_s = o_fg + N_HEADS
    wg = jnp.pad(w_in[:, o_fg:o_s], ((0, 0), (0, LANES - N_HEADS)))
    wgh, wgl = _split2(wg)
    wf = jnp.concatenate([w_in[:, :o_fg].astype(BF16), wgh], axis=1)
    ws = jnp.concatenate([w_in[:, o_s:].astype(BF16), wgl], axis=1)
    bfp = jnp.pad(b_forget.astype(F32), (0, LANES - N_HEADS))[None, :]
    ltri = jnp.asarray(np.tril(np.ones((tm, tm), np.float32)), BF16)
    bd = jnp.asarray(np.kron(np.eye(N_HEADS, dtype=np.float32),
                             np.full((HEAD_DIM, HEAD_DIM), 1.0 / HEAD_DIM, np.float32)), BF16)
    gq = (jnp.tile(q_gain.astype(F32), N_HEADS) * QK_SCALE)[None, :]
    gk = jnp.tile(k_gain.astype(F32), N_HEADS)[None, :]
    selq, selk, cq, ck, shift = _aug_tables()
    selq, selk = jnp.asarray(selq, BF16), jnp.asarray(selk, BF16)
    cq = jnp.asarray(cq)
    ck = jnp.asarray(ck) - bound.astype(BF16).astype(F32) * jnp.asarray(shift)

    def full(a):
        return pl.BlockSpec(a.shape, lambda b, i: (0,) * a.ndim)

    consts = (gain.astype(F32)[None, :], wf, ws, wgh, bfp, ltri, bd, gq, gk,
              selq, selk, cq, ck)
    head_shape = jax.ShapeDtypeStruct((bsz, N_HEADS, s_len, LANES), BF16)
    head_spec = pl.BlockSpec((1, N_HEADS, tm, LANES), lambda b, i: (b, 0, i, 0))
    return pl.pallas_call(
        _inproj_kernel,
        grid=(bsz, s_len // tm),
        in_specs=[pl.BlockSpec((1, tm, d), lambda b, i: (b, i, 0))] + [full(a) for a in consts],
        out_specs=[head_spec] * 6 + [
            pl.BlockSpec((1, tm, SSM_WIDTH), lambda b, i: (b, i, 0)),
            pl.BlockSpec((1, tm, LANES), lambda b, i: (b, i, 0))],
        out_shape=[head_shape] * 6 + [
            jax.ShapeDtypeStruct((bsz, s_len, SSM_WIDTH), F32),
            jax.ShapeDtypeStruct((bsz, s_len, LANES), F32)],
        scratch_shapes=[pltpu.VMEM((1, LANES), F32)],
        compiler_params=_cparams(("arbitrary", "arbitrary")),
        name="inproj",
    )(x, *consts)


def _pair_rms(y, gain):
    lane = lax.broadcasted_iota(jnp.int32, y.shape, 1)
    lo = lane < HEAD_DIM
    y2 = y * y
    ms0 = jnp.sum(jnp.where(lo, y2, 0.0), axis=-1, keepdims=True) * (1.0 / HEAD_DIM)
    ms1 = jnp.sum(jnp.where(lo, 0.0, y2), axis=-1, keepdims=True) * (1.0 / HEAD_DIM)
    return y * jnp.where(lo, lax.rsqrt(ms0 + EPS), lax.rsqrt(ms1 + EPS)) * gain


def _fox_kernel(kvs_ref, qa_ref, ka_ref, va_ref, gain_ref, o_ref, acc_ref, m_ref, *, nq, bounded):
    tq = qa_ref.shape[2]
    b, p, i = pl.program_id(0), pl.program_id(1), pl.program_id(2)
    row = lax.broadcasted_iota(jnp.int32, (tq, tq), 0)
    col = lax.broadcasted_iota(jnp.int32, (tq, tq), 1)

    def step(j, masked, heads):
        off = pl.multiple_of(j * tq, tq)
        ss = [_dot_nt(qa_ref[0, hh], ka_ref[0, hh, pl.ds(off, tq), :]) for hh in heads]
        if masked:
            ss = [jnp.where(col <= row, s, NEG_BIG) for s in ss]
        if bounded:
            ps = [jnp.exp(s).astype(BF16) for s in ss]
            for hh, pexp in zip(heads, ps):
                acc_ref[hh] += _dot(pexp, va_ref[0, hh, pl.ds(off, tq), :])
            return
        for hh, s in zip(heads, ss):
            m_prev = m_ref[hh]
            m_new = jnp.maximum(m_prev, jnp.max(s, axis=-1, keepdims=True))
            pexp = jnp.exp(s - m_new).astype(BF16)
            acc_ref[hh] = (jnp.exp(m_prev - m_new) * acc_ref[hh]
                           + _dot(pexp, va_ref[0, hh, pl.ds(off, tq), :]))
            m_ref[hh] = m_new

    def sweep(lo, hi, heads):
        def body(j, carry):
            step(j, False, heads)
            return carry

        lax.fori_loop(lo, hi, body, 0)

    starts = [kvs_ref[(b * N_HEADS + 2 * p + hh) * nq + i] for hh in range(2)]
    for hh in range(2):
        acc_ref[hh] = jnp.zeros((tq, LANES), F32)
        if not bounded:
            m_ref[hh] = jnp.full((tq, 1), NEG_BIG, F32)
    both = jnp.maximum(starts[0], starts[1])
    sweep(starts[0], both, (0,))
    sweep(starts[1], both, (1,))
    sweep(both, i, (0, 1))
    step(i, True, (0, 1))

    lane = lax.broadcasted_iota(jnp.int32, (tq, LANES), 1)
    a0, a1 = acc_ref[0], acc_ref[1]
    y = jnp.where(lane < HEAD_DIM, a0 / a0[:, HEAD_DIM:HEAD_DIM + 1], a1 / a1[:, 0:1])
    o_ref[0] = _pair_rms(y, gain_ref[0]).astype(BF16)


def _fox_call(kv_start, qa, ka, va, gain3, *, bounded):
    bsz, _, s_len, _ = qa.shape
    tq = TQ
    nq = s_len // tq
    kv_spec = pl.BlockSpec((1, 2, s_len, LANES), lambda b, p, i, kvs: (b, p, 0, 0))
    return pl.pallas_call(
        functools.partial(_fox_kernel, nq=nq, bounded=bounded),
        grid_spec=pltpu.PrefetchScalarGridSpec(
            num_scalar_prefetch=1,
            grid=(bsz, N_PAIRS, nq),
            in_specs=[pl.BlockSpec((1, 2, tq, LANES), lambda b, p, i, kvs: (b, p, i, 0)),
                      kv_spec, kv_spec,
                      pl.BlockSpec((1, 1, LANES), lambda b, p, i, kvs: (p, 0, 0))],
            out_specs=pl.BlockSpec((1, tq, LANES), lambda b, p, i, kvs: (b, i, p)),
            scratch_shapes=[pltpu.VMEM((2, tq, LANES), F32), pltpu.VMEM((2, tq, 1), F32)]),
        out_shape=jax.ShapeDtypeStruct((bsz, s_len, ATTN_WIDTH), BF16),
        compiler_params=_cparams(("arbitrary", "arbitrary", "arbitrary")),
        name="fox_attention_bounded" if bounded else "fox_attention",
    )(kv_start, qa, ka, va, gain3)


def _fox(qa, ka, va, kv_start, gain3, bound):
    return lax.cond(bound <= FOX_MAX_BOUND,
                    functools.partial(_fox_call, bounded=True),
                    functools.partial(_fox_call, bounded=False),
                    kv_start, qa, ka, va, gain3)


def _fox_bound(q_gain, k_gain):
    return 1.05 * HEAD_DIM * QK_SCALE * jnp.max(jnp.abs(q_gain)) * jnp.max(jnp.abs(k_gain))


def _fox_kv_start(c, bound, s_len):
    tq = TQ
    nq = s_len // tq
    ch = c[:, :, :N_HEADS]
    c_first = ch[:, 0::tq, :]
    c_end = ch[:, tq - 1::tq, :]
    skip = (c_first[:, :, None, :] - c_end[:, None, :, :]) < (LOG_ZERO - 2.0 * bound)
    start = jnp.sum(skip.astype(jnp.int32), axis=2)
    start = jnp.minimum(start, jnp.arange(nq, dtype=jnp.int32)[None, :, None])
    return start.transpose(0, 2, 1).reshape(-1)


def _sb_kernel(q_ref, k_ref, v_ref, u_ref, gain_ref, o_ref, acc_ref, r_ref):
    tq = q_ref.shape[2]
    blk = SB_SUB
    i = pl.program_id(2)
    row = lax.broadcasted_iota(jnp.int32, (blk, blk), 0)
    col = lax.broadcasted_iota(jnp.int32, (blk, blk), 1)
    chains = [(sub, hh) for sub in range(2) for hh in range(2)]

    def steps(todo, block_of, masked):
        u = u_ref[...]
        rows = [slice(sub * blk, (sub + 1) * blk) for sub, _ in todo]
        offs = [pl.multiple_of(block_of(sub) * blk, blk) for sub, _ in todo]
        zs = [_dot_nt(q_ref[0, hh, rw, :], k_ref[0, hh, pl.ds(off, blk), :])
              for (_, hh), rw, off in zip(todo, rows, offs)]
        sps = [jnp.maximum(z, 0.0) + jnp.log(1.0 + jnp.exp(-jnp.abs(z))) for z in zs]
        log_nots = [jnp.where(col < row, -sp, 0.0) if masked else -sp for sp in sps]
        laters = [_dot(ln.astype(BF16), u) for ln in log_nots]
        r_prevs = [r_ref[hh, rw, :] for (_, hh), rw in zip(todo, rows)]
        weights = [jnp.exp(z - sp + later + r_prev)
                   for z, sp, later, r_prev in zip(zs, sps, laters, r_prevs)]
        if masked:
            weights = [jnp.where(col < row, a, 0.0) for a in weights]
        r_max = None
        for (_, hh), rw, off, a, later, ln, r_prev in zip(todo, rows, offs, weights, laters,
                                                          log_nots, r_prevs):
            acc_ref[hh, rw, :] += _dot(a.astype(BF16), v_ref[0, hh, pl.ds(off, blk), :])
            r_new = r_prev + later[:, 0:1] + ln[:, 0:1]
            r_ref[hh, rw, :] = r_new
            m = jnp.max(r_new)
            r_max = m if r_max is None else jnp.maximum(r_max, m)
        return r_max

    def all_chains(block_of, masked):
        return steps(chains, block_of, masked)

    acc_ref[...] = jnp.zeros_like(acc_ref)
    r_ref[...] = jnp.zeros_like(r_ref)
    diag = 2 * i
    r_diag = all_chains(lambda sub: diag + sub, True)

    def cond(carry):
        n, r_max = carry
        return jnp.logical_and(n <= diag, r_max >= LOG_ZERO)

    def body(carry):
        n, _ = carry
        return n + 1, all_chains(lambda sub: diag + sub - n, False)

    n_end, r_end = lax.while_loop(cond, body, (1, r_diag))

    @pl.when(jnp.logical_and(n_end == diag + 1, r_end >= LOG_ZERO))
    def _():
        steps([(1, 0), (1, 1)], lambda sub: 0, False)

    lane = lax.broadcasted_iota(jnp.int32, (tq, LANES), 1)
    y = jnp.where(lane < HEAD_DIM, acc_ref[0], acc_ref[1])
    o_ref[0] = _pair_rms(y, gain_ref[0]).astype(BF16)


def _sb(sq, sk, sv, gain3):
    bsz, _, s_len, _ = sq.shape
    tq = TQ_SB
    upper =jnp.asarray(np.tril(np.ones((SB_SUB, SB_SUB), np.float32), -1), BF16)
    kv_spec = pl.BlockSpec((1, 2, s_len, LANES), lambda b, p, i: (b, p, 0, 0))
    return pl.pallas_call(
        _sb_kernel,
        grid=(bsz, N_PAIRS, s_len // tq),
        in_specs=[pl.BlockSpec((1, 2, tq, LANES), lambda b, p, i: (b, p, i, 0)),
                  kv_spec, kv_spec,
                  pl.BlockSpec((SB_SUB, SB_SUB), lambda b, p, i: (0, 0)),
                  pl.BlockSpec((1, 1, LANES), lambda b, p, i: (p, 0, 0))],
        out_specs=pl.BlockSpec((1, tq, LANES), lambda b, p, i: (b, i, p)),
        out_shape=jax.ShapeDtypeStruct((bsz, s_len, ATTN_WIDTH), BF16),
        scratch_shapes=[pltpu.VMEM((2, tq, LANES), F32), pltpu.VMEM((2, tq, 1), F32)],
        compiler_params=_cparams(("arbitrary", "arbitrary", "arbitrary")),
        name="stickbreak_attention",
    )(sq, sk, sv, upper, gain3)


def _ssm_kernel(u_ref, bre_ref, bim_ref, ccat_ref, pw_ref, d_ref, wglu_ref, bglu_ref,
                bd_ref, gain_ref, o_ref, xr_ref, xi_ref, hc_ref):
    chunk = u_ref.shape[1]

    @pl.when(pl.program_id(1) == 0)
    def _():
        hc_ref[...] = jnp.zeros_like(hc_ref)

    u = u_ref[0]
    ub = u.astype(BF16)
    xr_ref[...] = _dot(ub, bre_ref[...])
    xi_ref[...] = _dot(ub, bim_ref[...])

    def group(g, carry):
        hr, hi = carry
        rows = pl.ds(pl.multiple_of(g * 8, 8), 8)
        xr, xi = xr_ref[rows, :], xi_ref[rows, :]
        for k, d in enumerate((1, 2, 4)):
            mr, mi = pw_ref[2 + 2 * k], pw_ref[3 + 2 * k]
            sr, si = pltpu.roll(xr, d, 0), pltpu.roll(xi, d, 0)
            xr, xi = xr + mr * sr - mi * si, xi + mr * si + mi * sr
        pr, pi = pw_ref[0], pw_ref[1]
        xr, xi = xr + pr * hr - pi * hi, xi + pr * hi + pi * hr
        xr_ref[rows, :] = xr
        xi_ref[rows, :] = xi
        return xr[7:8, :], xi[7:8, :]

    hr, hi = lax.fori_loop(0, chunk // 8, group, (hc_ref[0:1, :], hc_ref[1:2, :]))
    hc_ref[0:1, :] = hr
    hc_ref[1:2, :] = hi

    hcat = jnp.concatenate([xr_ref[...], xi_ref[...]], axis=1).astype(BF16)
    y = _dot(hcat, ccat_ref[...]) + d_ref[...] * u
    g = 0.5 * y * (1.0 + jnp.tanh(math.sqrt(2.0 / math.pi) * (y + 0.044715 * (y * y * y))))
    g = g * jax.nn.sigmoid(_dot(g.astype(BF16), wglu_ref[...]) + bglu_ref[...])
    ms = _dot((g * g).astype(BF16), bd_ref[...])
    o_ref[0] = (g * lax.rsqrt(ms + EPS) * gain_ref[...]).astype(BF16)


def _ssm_tables(a_re, a_im, log_dt, b_re, b_im, c_re, c_im):
    a_re = a_re.astype(F32)
    a_im = a_im.astype(F32)
    dt = jnp.exp(log_dt.astype(F32))[:, None]
    mag = jnp.exp(dt * a_re)
    ang = dt * a_im
    abar_re = mag * jnp.cos(ang)
    abar_im = mag * jnp.sin(ang)
    num_re = abar_re - 1.0
    num_im = abar_im
    den = a_re * a_re + a_im * a_im
    coef_re = (num_re * a_re + num_im * a_im) / den
    coef_im = (num_im * a_re - num_re * a_im) / den
    b_re = b_re.astype(F32)
    b_im = b_im.astype(F32)
    bbar_re = coef_re[..., None] * b_re - coef_im[..., None] * b_im
    bbar_im = coef_re[..., None] * b_im + coef_im[..., None] * b_re
    eye = jnp.eye(SSM_GROUPS, dtype=F32)

    def b_blockdiag(bb):
        return jnp.einsum('gnc,gh->gchn', bb, eye).reshape(SSM_WIDTH, SSM_LANES).astype(BF16)

    def c_blockdiag(cc):
        return jnp.einsum('gcn,gh->gnhc', cc.astype(F32), eye).reshape(SSM_LANES, SSM_WIDTH)

    ccat = jnp.concatenate([c_blockdiag(c_re), -c_blockdiag(c_im)], axis=0).astype(BF16)

    ar = abar_re.reshape(1, SSM_LANES)
    ai = abar_im.reshape(1, SSM_LANES)
    pows = [(ar, ai)]
    for _ in range(7):
        pr, pi = pows[-1]
        pows.append((pr * ar - pi * ai, pr * ai + pi * ar))
    p_re = jnp.concatenate([p[0] for p in pows], axis=0)
    p_im = jnp.concatenate([p[1] for p in pows], axis=0)
    rows = jnp.arange(8)[:, None]
    tabs = [p_re, p_im]
    for d in (1, 2, 4):
        tabs.append(jnp.where(rows >= d, pows[d - 1][0], 0.0))
        tabs.append(jnp.where(rows >= d, pows[d - 1][1], 0.0))
    return b_blockdiag(bbar_re), b_blockdiag(bbar_im), ccat, jnp.stack(tabs, axis=0)


def _ssm(su, tables, d_skip, w_glu, b_glu, gain):
    bsz, s_len, _ = su.shape
    chunk = SSM_CHUNK
    bre, bim, ccat, pw = tables
    bd = jnp.asarray(np.kron(np.eye(SSM_GROUPS, dtype=np.float32),
                             np.full((SSM_GROUP, SSM_GROUP), 1.0 / SSM_GROUP, np.float32)), BF16)
    consts = (bre, bim, ccat, pw, d_skip.astype(F32)[None, :], w_glu.astype(BF16),
              b_glu.astype(F32)[None, :], bd, gain.astype(F32)[None, :])

    def full(a):
        return pl.BlockSpec(a.shape, lambda b, i: (0,) * a.ndim)

    return pl.pallas_call(
        _ssm_kernel,
        grid=(bsz, s_len // chunk),
        in_specs=[pl.BlockSpec((1, chunk, SSM_WIDTH), lambda b, i: (b, i, 0))]
        + [full(a) for a in consts],
        out_specs=pl.BlockSpec((1, chunk, SSM_WIDTH), lambda b, i: (b, i, 0)),
        out_shape=jax.ShapeDtypeStruct((bsz, s_len, SSM_WIDTH), BF16),
        scratch_shapes=[pltpu.VMEM((chunk, SSM_LANES), F32), pltpu.VMEM((chunk, SSM_LANES), F32),
                        pltpu.VMEM((2, SSM_LANES), F32)],
        compiler_params=_cparams(("arbitrary", "arbitrary")),
        name="s5_mixer",
    )(su, *consts)


def _outproj_kernel(*refs, routed):
    if routed:
        yf_ref, ys_ref, ym_ref, x_ref, wo_ref, g_ref, wrh_ref, wrl_ref, xo_ref, h_ref, rt_ref = refs
    else:
        yf_ref, ys_ref, ym_ref, x_ref, wo_ref, g_ref, xo_ref, h_ref = refs
    mix = jnp.concatenate([yf_ref[...], ys_ref[...], ym_ref[...]], axis=1)
    x = x_ref[...] + _dot(mix, wo_ref[...])
    xo_ref[...] = x
    h = x * lax.rsqrt(jnp.mean(x * x, axis=-1, keepdims=True) + EPS) * g_ref[...]
    if not routed:
        h_ref[...] = h.astype(BF16)
        return
    _store_token_tiles(h_ref, h)
    hb, hl = _split2(h)
    logits = _dot(hb, wrh_ref[...]) + _dot(hb, wrl_ref[...]) + _dot(hl, wrh_ref[...])
    lane = lax.broadcasted_iota(jnp.int32, logits.shape, 1).astype(F32)
    logits = jnp.where(lane < N_EXPERTS, logits, NEG_BIG)
    m1 = jnp.max(logits, axis=-1, keepdims=True)
    i1 = jnp.min(jnp.where(logits == m1, lane, float(LANES)), axis=-1, keepdims=True)
    rest = jnp.where(lane == i1, NEG_BIG, logits)
    m2 = jnp.max(rest, axis=-1, keepdims=True)
    i2 = jnp.min(jnp.where(rest == m2, lane, float(LANES)), axis=-1, keepdims=True)
    e2 = jnp.exp(m2 - m1)
    w1 = 1.0 / (1.0 + e2)
    w2 = e2 / (1.0 + e2)
    rt_ref[...] = jnp.where(lane == 0, i1, jnp.where(lane == 1, i2, jnp.where(
        lane == 2, w1, jnp.where(lane == 3, w2, 0.0))))


def _outproj(yf, ys, ym, x, w_out, ffn_gain, w_router=None):
    t, d = x.shape
    tm = TM_PROJ
    routed = w_router is not None
    consts = [w_out.astype(BF16), ffn_gain.astype(F32)[None, :]]
    if routed:
        consts += list(_split2(jnp.pad(w_router.astype(F32), ((0, 0), (0, LANES - N_EXPERTS)))))

    def rows(width):
        return pl.BlockSpec((tm, width), lambda i: (i, 0))

    def full(a):
        return pl.BlockSpec(a.shape, lambda i: (0,) * a.ndim)

    if routed:
        out_specs = [rows(d), pl.BlockSpec((tm * SUBLANES, LANES), lambda i: (i, 0)), rows(LANES)]
        out_shape = [jax.ShapeDtypeStruct((t, d), F32),
                     jax.ShapeDtypeStruct((t * SUBLANES, LANES), F32),
                     jax.ShapeDtypeStruct((t, LANES), F32)]
    else:
        out_specs = [rows(d), rows(d)]
        out_shape = [jax.ShapeDtypeStruct((t, d), F32), jax.ShapeDtypeStruct((t, d), BF16)]
    return pl.pallas_call(
        functools.partial(_outproj_kernel, routed=routed),
        grid=(t // tm,),
        in_specs=[rows(ATTN_WIDTH), rows(ATTN_WIDTH), rows(SSM_WIDTH), rows(d)]
        + [full(a) for a in consts],
        out_specs=out_specs,
        out_shape=out_shape,
        compiler_params=_cparams(("arbitrary",)),
        name="outproj_routed" if routed else "outproj",
    )(yf, ys, ym, x, *consts)


def _swiglu_chunk(h, wg_ref, wu_ref, wd_ref):
    g = _dot(h, wg_ref[...].astype(BF16))
    u = _dot(h, wu_ref[...].astype(BF16))
    return _dot((g * jax.nn.sigmoid(g) * u).astype(BF16), wd_ref[...].astype(BF16))


def _ffn_kernel(h_ref, x_ref, wg_ref, wu_ref, wd_ref, o_ref):
    @pl.when(pl.program_id(1) == 0)
    def _():
        o_ref[...] = x_ref[...]

    o_ref[...] += _swiglu_chunk(h_ref[...], wg_ref, wu_ref, wd_ref)


def _ffn(h, x, w_gate, w_up, w_down, layer):
    t, d = x.shape
    d_ff = w_gate.shape[2]
    tm, tf = TM_FFN, TF_FFN
    return pl.pallas_call(
        _ffn_kernel,
        grid=(t // tm, d_ff // tf),
        in_specs=[pl.BlockSpec((tm, d), lambda i, f: (i, 0)),
                  pl.BlockSpec((tm, d), lambda i, f: (i, 0)),
                  pl.BlockSpec((None, d, tf), lambda i, f: (layer, 0, f)),
                  pl.BlockSpec((None, d, tf), lambda i, f: (layer, 0, f)),
                  pl.BlockSpec((None, tf, d), lambda i, f: (layer, f, 0))],
        out_specs=pl.BlockSpec((tm, d), lambda i, f: (i, 0)),
        out_shape=jax.ShapeDtypeStruct((t, d), F32),
        compiler_params=_cparams(("arbitrary", "arbitrary")),
        name="dense_swiglu",
    )(h, x, w_gate, w_up, w_down)


def _tile_copy(src_hbm, dst_ref, src_tok, dst_tok, sem):
    return pltpu.make_async_copy(
        src_hbm.at[pl.ds(pl.multiple_of(src_tok * SUBLANES, SUBLANES), SUBLANES), :],
        dst_ref.at[pl.ds(pl.multiple_of(dst_tok * SUBLANES, SUBLANES), SUBLANES), :], sem)


def _start_gather(idx_ref, src_hbm, dst_ref, sem, n_tok):
    def body(r, carry):
        _tile_copy(src_hbm, dst_ref, idx_ref[0, 0, r], r, sem).start()
        return carry

    lax.fori_loop(0, n_tok, body, 0, unroll=8)


def _wait_gather(idx_ref, src_hbm, dst_ref, sem, n_tok):
    def body(r, carry):
        _tile_copy(src_hbm, dst_ref, idx_ref[0, 0, r], r, sem).wait()
        return carry

    lax.fori_loop(0, n_tok, body, 0, unroll=8)


def _grouped_ffn_kernel(te_ref, tv_ref, idx_ref, idx_next_ref, h_hbm, wg_ref, wu_ref, wd_ref,
                        o_ref, xs_ref, hb_ref, acc_ref, sem):
    i, f = pl.program_id(0), pl.program_id(1)
    n_tiles, nf = pl.num_programs(0), pl.num_programs(1)
    tm = hb_ref.shape[0]
    slot = i % 2
    live = tv_ref[i] > 0

    @pl.when(jnp.logical_and(i == 0, f == 0))
    def _():
        _start_gather(idx_ref, h_hbm, xs_ref.at[0], sem.at[0], tm)

    @pl.when(jnp.logical_and(f == 0, live))
    def _():
        _wait_gather(idx_ref, h_hbm, xs_ref.at[slot], sem.at[slot], tm)
        for s, part in enumerate(_load_token_tiles(xs_ref.at[slot], tm)):
            hb_ref[:, LANES * s:LANES * (s + 1)] = part.astype(BF16)

    @pl.when(jnp.logical_and(f == 0, tv_ref[jnp.minimum(i + 1, n_tiles - 1)] > 0))
    def _():
        @pl.when(i + 1 < n_tiles)
        def _():
            _start_gather(idx_next_ref, h_hbm, xs_ref.at[1 - slot], sem.at[1 - slot], tm)

    @pl.when(f == 0)
    def _():
        acc_ref[...] = jnp.zeros_like(acc_ref)

    @pl.when(live)
    def _():
        acc_ref[...] += _swiglu_chunk(hb_ref[...], wg_ref, wu_ref, wd_ref)

    @pl.when(f == nf - 1)
    def _():
        _store_token_tiles(o_ref, acc_ref[...])


def _moe_grouped_ffn(h_tiles, src_tok, tile_expert, tile_valid, w_gate, w_up, w_down, layer):
    n_tiles = tile_expert.shape[0]
    d, d_ff = w_gate.shape[2], w_gate.shape[3]
    tm, tf = TM_FFN, TF_FFN
    nf = d_ff // tf
    idx = src_tok.reshape(n_tiles, 1, tm)

    def f_idx(i, f, te, tv):
        return jnp.where(tv[i] > 0, f, nf - 1)

    return pl.pallas_call(
        _grouped_ffn_kernel,
        grid_spec=pltpu.PrefetchScalarGridSpec(
            num_scalar_prefetch=2,
            grid=(n_tiles, nf),
            in_specs=[pl.BlockSpec((1, 1, tm), lambda i, f, te, tv: (i, 0, 0), memory_space=pltpu.SMEM),
                      pl.BlockSpec((1, 1, tm), lambda i, f, te, tv: (jnp.minimum(i + 1, n_tiles - 1), 0, 0),
                                   memory_space=pltpu.SMEM),
                      pl.BlockSpec(memory_space=pl.ANY),
                      pl.BlockSpec((None, None, d, tf),
                                   lambda i, f, te, tv: (layer, te[i], 0, f_idx(i, f, te, tv))),
                      pl.BlockSpec((None, None, d, tf),
                                   lambda i, f, te, tv: (layer, te[i], 0, f_idx(i, f, te, tv))),
                      pl.BlockSpec((None, None, tf, d),
                                   lambda i, f, te, tv: (layer, te[i], f_idx(i, f, te, tv), 0))],
            out_specs=pl.BlockSpec((tm * SUBLANES, LANES), lambda i, f, te, tv: (i, 0)),
            scratch_shapes=[pltpu.VMEM((2, tm * SUBLANES, LANES), F32), pltpu.VMEM((tm, d), BF16),
                            pltpu.VMEM((tm, d), F32), pltpu.SemaphoreType.DMA((2,))]),
        out_shape=jax.ShapeDtypeStruct((n_tiles * tm * SUBLANES, LANES), F32),
        compiler_params=_cparams(("arbitrary", "arbitrary")),
        name="moe_grouped_swiglu",
    )(tile_expert, tile_valid, idx, idx, h_tiles, w_gate, w_up, w_down)


def _combine_kernel(pos_ref, pos_next_ref, ys_hbm, x_ref, rt_ref, o_ref, buf_ref, sem):
    i, n_steps = pl.program_id(0), pl.num_programs(0)
    tc = x_ref.shape[0]
    slot = i % 2

    @pl.when(i == 0)
    def _():
        _start_gather(pos_ref, ys_hbm, buf_ref.at[0], sem.at[0], 2 * tc)

    @pl.when(i + 1 < n_steps)
    def _():
        _start_gather(pos_next_ref, ys_hbm, buf_ref.at[1 - slot], sem.at[1 - slot], 2 * tc)

    _wait_gather(pos_ref, ys_hbm, buf_ref.at[slot], sem.at[slot], 2 * tc)
    rt = rt_ref[...]
    w0, w1 = rt[:, 2:3], rt[:, 3:4]
    y0 = _load_token_tiles(buf_ref.at[slot], tc)
    y1 = _load_token_tiles(buf_ref.at[slot], tc, base=tc * SUBLANES)
    for s in range(SUBLANES):
        cols = slice(LANES * s, LANES * (s + 1))
        o_ref[:, cols] = x_ref[:, cols] + w0 * y0[s] + w1 * y1[s]


def _moe_combine(ys, pos, x, route):
    t, d = x.shape
    tc = TC_COMBINE
    n_steps = t // tc
    return pl.pallas_call(
        _combine_kernel,
        grid=(n_steps,),
        in_specs=[pl.BlockSpec((1, 1, 2 * tc), lambda i: (i, 0, 0), memory_space=pltpu.SMEM),
                  pl.BlockSpec((1, 1, 2 * tc), lambda i: (jnp.minimum(i + 1, n_steps - 1), 0, 0),
                               memory_space=pltpu.SMEM),
                  pl.BlockSpec(memory_space=pl.ANY),
                  pl.BlockSpec((tc, d), lambda i: (i, 0)),
                  pl.BlockSpec((tc, LANES), lambda i: (i, 0))],
        out_specs=pl.BlockSpec((tc, d), lambda i: (i, 0)),
        out_shape=jax.ShapeDtypeStruct((t, d), F32),
        scratch_shapes=[pltpu.VMEM((2, 2 * tc * SUBLANES, LANES), F32), pltpu.SemaphoreType.DMA((2,))],
        compiler_params=_cparams(("arbitrary",)),
        name="moe_combine",
    )(pos, pos, ys, x, route)


def _moe(h, x, route, w_gate, w_up, w_down, layer):
    t = x.shape[0]
    tm, tc = TM_FFN, TC_COMBINE
    n_assign = TOP_K * t
    n_rows = n_assign + N_EXPERTS * tm
    expert = route[:, 0:TOP_K].astype(jnp.int32).reshape(-1)
    onehot = (expert[:, None] == jnp.arange(N_EXPERTS, dtype=jnp.int32)[None, :]).astype(jnp.int32)
    rank = jnp.sum((jnp.cumsum(onehot, axis=0) - onehot) * onehot, axis=1)
    counts = jnp.sum(onehot, axis=0)
    padded = ((counts + tm - 1) // tm) * tm
    group_end = jnp.cumsum(padded)
    dest = (group_end - padded)[expert] + rank
    src_tok = jnp.zeros((n_rows,), jnp.int32).at[dest].set(jnp.arange(n_assign, dtype=jnp.int32) // TOP_K)
    tile_first = jnp.arange(n_rows // tm, dtype=jnp.int32) * tm
    tile_valid = (tile_first < group_end[-1]).astype(jnp.int32)
    tile_expert = jnp.minimum(jnp.searchsorted(group_end, tile_first, side='right'),
                              N_EXPERTS - 1).astype(jnp.int32)
    last_expert = jnp.max(jnp.where(tile_valid > 0, tile_expert, 0))
    tile_expert = jnp.where(tile_valid > 0, tile_expert, last_expert)
    pos = dest.reshape(t // tc, tc, TOP_K).transpose(0, 2, 1).reshape(t // tc, 1, TOP_K * tc)

    ys = _moe_grouped_ffn(h, src_tok, tile_expert, tile_valid, w_gate, w_up, w_down, layer)
    return _moe_combine(ys, pos, x, route)


@jax.jit
def _forward(x, attn_norm, ffn_norm, w_in, b_forget, fox_q_norm, fox_k_norm,
             ssm_a_re, ssm_a_im, ssm_log_dt, ssm_b_re, ssm_b_im, ssm_c_re,
             ssm_c_im, ssm_d, ssm_w_glu, ssm_b_glu, out_norm, w_out,
             dense_w_gate, dense_w_up, dense_w_down, moe_w_router,
             moe_w_gate, moe_w_up, moe_w_down):
    bsz, s_len, d = x.shape
    depth = w_in.shape[0]
    x = x.astype(F32)
    for i in range(depth):
        bound = _fox_bound(fox_q_norm[i], fox_k_norm[i])
        qa, ka, va, sq, sk, sv, su, c = _inproj(x, attn_norm[i], w_in[i], b_forget[i],
                                                fox_q_norm[i], fox_k_norm[i], bound)
        gain = out_norm[i].astype(F32)
        kv_start = _fox_kv_start(c, bound, s_len)
        y_fox = _fox(qa, ka, va, kv_start, gain[:ATTN_WIDTH].reshape(N_PAIRS, 1, LANES), bound)
        y_sb = _sb(sq, sk, sv, gain[ATTN_WIDTH:2 * ATTN_WIDTH].reshape(N_PAIRS, 1, LANES))
        tables = _ssm_tables(ssm_a_re[i], ssm_a_im[i], ssm_log_dt[i], ssm_b_re[i], ssm_b_im[i],
                             ssm_c_re[i], ssm_c_im[i])
        y_ssm = _ssm(su, tables, ssm_d[i], ssm_w_glu[i], ssm_b_glu[i], gain[2 * ATTN_WIDTH:])
        t = bsz * s_len
        flat = (y_fox.reshape(t, -1), y_sb.reshape(t, -1), y_ssm.reshape(t, -1), x.reshape(t, d))
        j = i // 2
        if i % 2 == 0:
            xo, h = _outproj(*flat, w_out[i], ffn_norm[i])
            x = _ffn(h, xo, dense_w_gate, dense_w_up, dense_w_down, j)
        else:
            xo, h, route = _outproj(*flat, w_out[i], ffn_norm[i], moe_w_router[j])
            x = _moe(h, xo, route, moe_w_gate, moe_w_up, moe_w_down, j)
        x = x.reshape(bsz, s_len, d)
    return x


def kernel(x, attn_norm, ffn_norm, w_in, b_forget, fox_q_norm, fox_k_norm, ssm_a_re, ssm_a_im,
           ssm_log_dt, ssm_b_re, ssm_b_im, ssm_c_re, ssm_c_im, ssm_d, ssm_w_glu, ssm_b_glu,
           out_norm, w_out, dense_w_gate, dense_w_up, dense_w_down, moe_w_router, moe_w_gate,
           moe_w_up, moe_w_down):
    return _forward(x, attn_norm, ffn_norm, w_in, b_forget, fox_q_norm, fox_k_norm, ssm_a_re,
                    ssm_a_im, ssm_log_dt, ssm_b_re, ssm_b_im, ssm_c_re, ssm_c_im, ssm_d,
                    ssm_w_glu, ssm_b_glu, out_norm, w_out, dense_w_gate, dense_w_up,
                    dense_w_down, moe_w_router, moe_w_gate, moe_w_up, moe_w_down)
```
